```python
import math
import jax, jax.numpy as jnp
from jax import lax
import numpy as np

D_MODEL = 2048
BATCH = 1
SEQ = 8192
DEPTH = 4

GRID_W = 64
CTX_LEN = 256
N_MIXERS = 4
ROPE_BASE = 10000.0
EPS = 1e-6
NEG_INF = -1e30
BLOCK = 128
HEAD_DIM = 128

SWA_HEADS = D_MODEL // HEAD_DIM
SWA_KV_HEADS = SWA_HEADS // 4
SWA_WINDOW = BLOCK
SWA_WIDTH = SWA_HEADS * HEAD_DIM
SWA_IN = SWA_WIDTH + 2 * SWA_KV_HEADS * HEAD_DIM + SWA_WIDTH

MLA_HEADS = D_MODEL // HEAD_DIM
MLA_Q_RANK = 512
MLA_KV_RANK = 256
MLA_NOPE = 128
MLA_ROPE = 64
MLA_V = 128
MLA_WIDTH = MLA_HEADS * MLA_V
MLA_IN = MLA_Q_RANK + MLA_KV_RANK + MLA_ROPE + MLA_WIDTH

HYENA_WIDTH = D_MODEL
HYENA_ORDER = 2
HYENA_BANDS = 16
HYENA_EMB = 1 + 2 * HYENA_BANDS
HYENA_HIDDEN = 64
HYENA_CONV = 3
HYENA_DECAY_TARGET = 1e-2
HYENA_FAST_DECAY = 0.3
HYENA_SLOW_DECAY = 1.5
HYENA_IN = (HYENA_ORDER + 1) * HYENA_WIDTH + HYENA_WIDTH

DIFF_HEADS = D_MODEL // (2 * HEAD_DIM)
DIFF_WIDTH = DIFF_HEADS * 2 * HEAD_DIM
DIFF_IN = 4 * DIFF_WIDTH

kernel_name = 'hybrid_diffusion_interleaved_swa_mla_hyena_diff'


def rmsnorm(x, g):
    xf = x.astype(jnp.float32)
    y = xf * lax.rsqrt(jnp.mean(xf * xf, axis=-1, keepdims=True) + EPS)
    return (y * g.astype(jnp.float32)).astype(x.dtype)


def axial_rope_tables(n_tok, rot_dim):
    rows = n_tok // GRID_W
    row = jnp.repeat(jnp.arange(rows, dtype=jnp.float32), GRID_W)
    col = jnp.tile(jnp.arange(GRID_W, dtype=jnp.float32), rows)
    half = rot_dim // 2
    inv = 1.0 / (ROPE_BASE ** (jnp.arange(0, half, 2, dtype=jnp.float32) / half))
    ar = row[:, None] * inv[None, :]
    ac = col[:, None] * inv[None, :]
    ang = jnp.concatenate([ar, ar, ac, ac], axis=-1)
    return jnp.cos(ang), jnp.sin(ang)


def apply_axial_rope(x, cos, sin):
    half = x.shape[-1] // 2
    qr = half // 2
    a, b = x[..., :half], x[..., half:]
    rot = jnp.concatenate([-a[..., qr:], a[..., :qr], -b[..., qr:], b[..., :qr]], axis=-1)
    return x * cos[:, None].astype(x.dtype) + rot * sin[:, None].astype(x.dtype)


def sweep_query_blocks(fn, *qs):
    B, N = qs[0].shape[:2]
    nb = N // BLOCK
    blocks = tuple(jnp.moveaxis(q.reshape((B, nb, BLOCK) + q.shape[2:]), 1, 0) for q in qs)
    out = lax.map(lambda a: fn(*a), blocks)
    return jnp.moveaxis(out, 0, 1).reshape((B, N) + out.shape[3:])


def swa_mixer(h, hc, w_in, q_g, k_g, sink, w_out, need_ctx):
    B, N, _ = h.shape
    C = hc.shape[1]
    Hq, Hk, d = SWA_HEADS, SWA_KV_HEADS, HEAD_DIM
    G = Hq // Hk
    cuts = [Hq * d, Hq * d + Hk * d, Hq * d + 2 * Hk * d]
    q, k, v, gate = jnp.split(h @ w_in, cuts, axis=-1)
    cos, sin = axial_rope_tables(N, d)
    q = apply_axial_rope(rmsnorm(q.reshape(B, N, Hq, d), q_g), cos, sin)
    k = apply_axial_rope(rmsnorm(k.reshape(B, N, Hk, d), k_g), cos, sin)
    v = v.reshape(B, N, Hk, d)
    if need_ctx:
        cq, ck, cv, cgate = jnp.split(hc @ w_in, cuts, axis=-1)
    else:
        ck, cv = jnp.split(hc @ w_in[:, cuts[0]:cuts[2]], 2, axis=-1)
    ck = rmsnorm(ck.reshape(B, C, Hk, d), k_g)
    cv = cv.reshape(B, C, Hk, d)
    scale = d ** -0.5
    sink_l = sink.astype(jnp.float32).reshape(Hk, G)
    nb = N // BLOCK

    def bands(t):
        tp = jnp.pad(t, ((0, 0), (BLOCK, BLOCK), (0, 0), (0, 0))).reshape(B, nb + 2, BLOCK, Hk, d)
        return jnp.concatenate([tp[:, :-2], tp[:, 1:-1], tp[:, 2:]], axis=2)

    kb, vb = bands(k), bands(v)
    qb = q.reshape(B, nb, BLOCK, Hk, G, d)
    qi = jnp.arange(BLOCK)[:, None]
    kj = jnp.arange(3 * BLOCK)[None, :]
    in_band = (kj >= qi + BLOCK - SWA_WINDOW) & (kj <= qi + BLOCK + SWA_WINDOW)
    kpos = (jnp.arange(nb)[:, None] - 1) * BLOCK + jnp.arange(3 * BLOCK)[None, :]
    mask = in_band[None] & ((kpos >= 0) & (kpos < N))[:, None, :]
    s_loc = jnp.einsum('bnqhgd,bnkhd->bnhgqk', qb, kb).astype(jnp.float32) * scale
    s_loc = jnp.where(mask[None, :, None, None], s_loc, NEG_INF)
    s_ctx = jnp.einsum('bnqhgd,bchd->bnhgqc', qb, ck).astype(jnp.float32) * scale
    s_sink = jnp.broadcast_to(sink_l[None, None, :, :, None, None], s_ctx.shape[:-1] + (1,))
    p = jax.nn.softmax(jnp.concatenate([s_loc, s_ctx, s_sink], axis=-1), axis=-1).astype(v.dtype)
    o = (jnp.einsum('bnhgqk,bnkhd->bnqhgd', p[..., :3 * BLOCK], vb)
         + jnp.einsum('bnhgqc,bchd->bnqhgd', p[..., 3 * BLOCK:3 * BLOCK + C], cv)).reshape(B, N, Hq * d)
    out = (o * jax.nn.silu(gate)) @ w_out
    out_c = None
    if need_ctx:
        cqh = rmsnorm(cq.reshape(B, C, Hq, d), q_g).reshape(B, C, Hk, G, d)
        sc = jnp.einsum('bqhgd,bkhd->bhgqk', cqh, ck).astype(jnp.float32) * scale
        sc_sink = jnp.broadcast_to(sink_l[None, :, :, None, None], sc.shape[:-1] + (1,))
        pc = jax.nn.softmax(jnp.concatenate([sc, sc_sink], axis=-1), axis=-1)[..., :C].astype(cv.dtype)
        oc = jnp.einsum('bhgqk,bkhd->bqhgd', pc, cv).reshape(B, C, Hq * d)
        out_c = (oc * jax.nn.silu(cgate)) @ w_out
    return out, out_c


def mla_attend(qn, qp, kn, kp, v):
    scale = (MLA_NOPE + MLA_ROPE) ** -0.5
    s = (jnp.einsum('bqhd,bkhd->bhqk', qn, kn) + jnp.einsum('bqhr,bkr->bhqk', qp, kp)).astype(jnp.float32) * scale
    p = jax.nn.softmax(s, axis=-1).astype(v.dtype)
    return jnp.einsum('bhqk,bkhd->bqhd', p, v)


def mla_mixer(h, hc, w_in, qa_g, kva_g, w_qb, w_kvb, qn_nope_g, qn_pe_g, kn_nope_g, kn_pe_g, w_out, need_ctx):
    B, N, _ = h.shape
    C = hc.shape[1]
    H = MLA_HEADS
    cuts = [MLA_Q_RANK, MLA_Q_RANK + MLA_KV_RANK, MLA_Q_RANK + MLA_KV_RANK + MLA_ROPE]

    def queries(c_q):
        S = c_q.shape[1]
        qf = (rmsnorm(c_q, qa_g) @ w_qb).reshape(B, S, H, MLA_NOPE + MLA_ROPE)
        return rmsnorm(qf[..., :MLA_NOPE], qn_nope_g), rmsnorm(qf[..., MLA_NOPE:], qn_pe_g)

    def keys_values(c_kv, k_rope):
        S = c_kv.shape[1]
        kv = (rmsnorm(c_kv, kva_g) @ w_kvb).reshape(B, S, H, MLA_NOPE + MLA_V)
        return rmsnorm(kv[..., :MLA_NOPE], kn_nope_g), rmsnorm(k_rope, kn_pe_g), kv[..., MLA_NOPE:]

    lat_cq, lat_ckv, lat_kr, gate = jnp.split(h @ w_in, cuts, axis=-1)
    q_nope, q_pe = queries(lat_cq)
    k_nope, k_pe, v = keys_values(lat_ckv, lat_kr)
    cos, sin = axial_rope_tables(N, MLA_ROPE)
    q_pe = apply_axial_rope(q_pe, cos, sin)
    k_pe = apply_axial_rope(k_pe[:, :, None], cos, sin)[:, :, 0]
    if need_ctx:
        ctx_cq, ctx_ckv, ctx_kr, cgate = jnp.split(hc @ w_in, cuts, axis=-1)
    else:
        ctx_ckv, ctx_kr = jnp.split(hc @ w_in[:, cuts[0]:cuts[2]], [MLA_KV_RANK], axis=-1)
    ck_nope, ck_pe, cv = keys_values(ctx_ckv, ctx_kr)
    kn_all = jnp.concatenate([k_nope, ck_nope], axis=1)
    kp_all = jnp.concatenate([k_pe, ck_pe], axis=1)
    v_all = jnp.concatenate([v, cv], axis=1)
    o = sweep_query_blocks(lambda a, b: mla_attend(a, b, kn_all, kp_all, v_all), q_nope, q_pe)
    out = (o.reshape(B, N, MLA_WIDTH) * jax.nn.silu(gate)) @ w_out
    out_c = None
    if need_ctx:
        cq_nope, cq_pe = queries(ctx_cq)
        oc = mla_attend(cq_nope, cq_pe, ck_nope, ck_pe, cv).reshape(B, C, MLA_WIDTH)
        out_c = (oc * jax.nn.silu(cgate)) @ w_out
    return out, out_c


def centred_conv3(u, w, b):
    up = jnp.pad(u, ((0, 0), (1, 1), (0, 0)))
    return up[:, :-2] * w[0] + up[:, 1:-1] * w[1] + up[:, 2:] * w[2] + b


def hyena_filter_spectrum(L, f_w1, f_b1, f_w2, f_b2, f_freq, f_w3):
    f32 = jnp.float32
    pos = jnp.arange(L, dtype=f32)
    t = pos / max(L - 1, 1)
    bands = jnp.linspace(1e-4, HYENA_BANDS - 1, HYENA_BANDS, dtype=f32)
    ang = (2.0 * math.pi / L) * pos[:, None] * bands[None, :]
    z = jnp.concatenate([t[:, None], jnp.cos(ang), -jnp.sin(ang)], axis=-1)
    freq = f_freq.astype(f32)
    hf = jnp.sin(freq[0] * (z @ f_w1.astype(f32) + f_b1.astype(f32)))
    hf = jnp.sin(freq[1] * (hf @ f_w2.astype(f32) + f_b2.astype(f32)))
    hf = (hf @ f_w3.astype(f32)).reshape(L, HYENA_ORDER, 2, HYENA_WIDTH)
    max_decay = math.log(HYENA_DECAY_TARGET) / HYENA_FAST_DECAY
    min_decay = math.log(HYENA_DECAY_TARGET) / HYENA_SLOW_DECAY
    deltas = jnp.abs(jnp.linspace(min_decay, max_decay, HYENA_WIDTH, dtype=f32))
    hf = hf * jnp.exp(-t[:, None] * deltas[None, :])[:, None, None, :]
    fwd = hf[:, :, 0]
    bwd = hf[:0:-1, :, 1]
    kc = jnp.concatenate([fwd, jnp.zeros((1, HYENA_ORDER, HYENA_WIDTH), f32), bwd], axis=0)
    kc = kc * lax.rsqrt(jnp.sum(kc * kc, axis=0, keepdims=True) + EPS)
    return jnp.fft.rfft(kc, axis=0)


def bidir_fftconv(u, spec, skip):
    L = u.shape[1]
    y = jnp.fft.irfft(jnp.fft.rfft(u, n=2 * L, axis=1) * spec[None], n=2 * L, axis=1)[:, :L]
    return y + u * skip


def hyena_branch(hin, w_in, conv_w, conv_b, f_w1, f_b1, f_w2, f_b2, f_freq, f_w3, skip, w_out):
    L = hin.shape[1]
    proj = hin @ w_in
    u, gate = proj[..., :(HYENA_ORDER + 1) * HYENA_WIDTH], proj[..., (HYENA_ORDER + 1) * HYENA_WIDTH:]
    u = centred_conv3(u, conv_w, conv_b).astype(jnp.float32)
    v, x1, x2 = jnp.split(u, 3, axis=-1)
    spec = hyena_filter_spectrum(L, f_w1, f_b1, f_w2, f_b2, f_freq, f_w3)
    sk = skip.astype(jnp.float32)
    z = x1 * bidir_fftconv(v, spec[:, 0], sk[0])
    z = x2 * bidir_fftconv(z, spec[:, 1], sk[1])
    return (z.astype(hin.dtype) * jax.nn.silu(gate)) @ w_out


def diff_attend(q, k, v, lam):
    s = jnp.einsum('bqhid,bkhid->bihqk', q, k).astype(jnp.float32) * (HEAD_DIM ** -0.5)
    p = jax.nn.softmax(s, axis=-1)
    a = (p[:, 0] - lam * p[:, 1]).astype(v.dtype)
    return jnp.einsum('bhqk,bkhe->bqhe', a, v)


def diff_mixer(h, hc, w_in, q_g, k_g, lq1, lk1, lq2, lk2, subln_g, w_out, lam_init, need_ctx):
    B, N, _ = h.shape
    C = hc.shape[1]
    H, d, W = DIFF_HEADS, HEAD_DIM, DIFF_WIDTH
    f32 = jnp.float32

    def qk(t, g):
        return rmsnorm(t.reshape(B, t.shape[1], H, 2, d), g)

    q, k, v, gate = jnp.split(h @ w_in, 4, axis=-1)
    cos, sin = axial_rope_tables(N, d)

    def rope2(t):
        return apply_axial_rope(t.reshape(B, N, 2 * H, d), cos, sin).reshape(B, N, H, 2, d)

    q = rope2(qk(q, q_g))
    k = rope2(qk(k, k_g))
    v = v.reshape(B, N, H, 2 * d)
    if need_ctx:
        cq, ck, cv, cgate = jnp.split(hc @ w_in, 4, axis=-1)
    else:
        ck, cv = jnp.split(hc @ w_in[:, W:3 * W], 2, axis=-1)
    ck = qk(ck, k_g)
    cv = cv.reshape(B, C, H, 2 * d)
    lam = (jnp.exp(jnp.sum(lq1.astype(f32) * lk1.astype(f32)))
           - jnp.exp(jnp.sum(lq2.astype(f32) * lk2.astype(f32))) + lam_init)
    k_all = jnp.concatenate([k, ck], axis=1)
    v_all = jnp.concatenate([v, cv], axis=1)

    def finish(o, g):
        o = (rmsnorm(o, subln_g) * (1.0 - lam_init)).reshape(B, o.shape[1], W)
        return (o * jax.nn.silu(g)) @ w_out

    o = sweep_query_blocks(lambda qb: diff_attend(qb, k_all, v_all, lam), q)
    out = finish(o, gate)
    out_c = None
    if need_ctx:
        out_c = finish(diff_attend(qk(cq, q_g), ck, cv, lam), cgate)
    return out, out_c


def setup_inputs(seed: int = 0) -> dict:
    key = jax.random.key(seed)
    ks = iter(jax.random.split(key, 64))
    f32 = jnp.float32

    def nrm(shape, std):
        return jax.random.normal(next(ks), shape, f32) * std

    def gain(shape):
        return 1.0 + nrm(shape, 0.05)

    nA, nB, nC, nD = (len(range(m, DEPTH, N_MIXERS)) for m in range(N_MIXERS))
    D = D_MODEL
    d = HEAD_DIM
    W = HYENA_WIDTH
    return {
        'x': nrm((BATCH, SEQ, D), 1.0),
        'c': nrm((BATCH, D), 1.0),
        'ctx': nrm((BATCH, CTX_LEN, D), 1.0),
        'c_ctx': nrm((D,), 1.0),
        'norm_g': gain((DEPTH, D)),
        'ada_w': nrm((DEPTH, D, 3 * D), 0.5 * D ** -0.5),
        'ada_b': nrm((DEPTH, 3 * D), 0.02),
        'swa_w_in': nrm((nA, D, SWA_IN), D ** -0.5),
        'swa_q_g': gain((nA, d)),
        'swa_k_g': gain((nA, d)),
        'swa_sink': nrm((nA, SWA_HEADS), 1.0),
        'swa_w_out': nrm((nA, SWA_WIDTH, D), SWA_WIDTH ** -0.5),
        'mla_w_in': nrm((nB, D, MLA_IN), D ** -0.5),
        'mla_qa_g': gain((nB, MLA_Q_RANK)),
        'mla_kva_g': gain((nB, MLA_KV_RANK)),
        'mla_w_qb': nrm((nB, MLA_Q_RANK, MLA_HEADS * (MLA_NOPE + MLA_ROPE)), MLA_Q_RANK ** -0.5),
        'mla_w_kvb': nrm((nB, MLA_KV_RANK, MLA_HEADS * (MLA_NOPE + MLA_V)), MLA_KV_RANK ** -0.5),
        'mla_qn_nope_g': gain((nB, MLA_NOPE)),
        'mla_qn_pe_g': gain((nB, MLA_ROPE)),
        'mla_kn_nope_g': gain((nB, MLA_NOPE)),
        'mla_kn_pe_g': gain((nB, MLA_ROPE)),
        'mla_w_out': nrm((nB, MLA_WIDTH, D), MLA_WIDTH ** -0.5),
        'hyena_w_in': nrm((nC, D, HYENA_IN), D ** -0.5),
        'hyena_conv_w': nrm((nC, HYENA_CONV, (HYENA_ORDER + 1) * W), 0.5),
        'hyena_conv_b': nrm((nC, (HYENA_ORDER + 1) * W), 0.02),
        'hyena_f_w1': nrm((nC, HYENA_EMB, HYENA_HIDDEN), HYENA_EMB ** -0.5),
        'hyena_f_b1': nrm((nC, HYENA_HIDDEN), 0.2),
        'hyena_f_w2': nrm((nC, HYENA_HIDDEN, HYENA_HIDDEN), HYENA_HIDDEN ** -0.5),
        'hyena_f_b2': nrm((nC, HYENA_HIDDEN), 0.2),
        'hyena_f_freq': gain((nC, 2, HYENA_HIDDEN)),
        'hyena_f_w3': nrm((nC, HYENA_HIDDEN, HYENA_ORDER * 2 * W), HYENA_HIDDEN ** -0.5),
        'hyena_skip': nrm((nC, HYENA_ORDER, W), 0.5),
        'hyena_w_out': nrm((nC, W, D), W ** -0.5),
        'diff_w_in': nrm((nD, D, DIFF_IN), D ** -0.5),
        'diff_q_g': gain((nD, d)),
        'diff_k_g': gain((nD, d)),
        'diff_lq1': nrm((nD, d), 0.1),
        'diff_lk1': nrm((nD, d), 0.1),
        'diff_lq2': nrm((nD, d), 0.1),
        'diff_lk2': nrm((nD, d), 0.1),
        'diff_subln_g': gain((nD, 2 * d)),
        'diff_w_out': nrm((nD, DIFF_WIDTH, D), DIFF_WIDTH ** -0.5),
    }


def reference(x, c, ctx, c_ctx, norm_g, ada_w, ada_b,
              swa_w_in, swa_q_g, swa_k_g, swa_sink, swa_w_out,
              mla_w_in, mla_qa_g, mla_kva_g, mla_w_qb, mla_w_kvb,
              mla_qn_nope_g, mla_qn_pe_g, mla_kn_nope_g, mla_kn_pe_g, mla_w_out,
              hyena_w_in, hyena_conv_w, hyena_conv_b, hyena_f_w1, hyena_f_b1, hyena_f_w2,
              hyena_f_b2, hyena_f_freq, hyena_f_w3, hyena_skip, hyena_w_out,
              diff_w_in, diff_q_g, diff_k_g, diff_lq1, diff_lk1, diff_lq2, diff_lk2,
              diff_subln_g, diff_w_out):
    cond = jax.nn.silu(c)[:, None, :]
    cond_ctx = jax.nn.silu(c_ctx)
    for i in range(DEPTH):
        kind, j = i % N_MIXERS, i // N_MIXERS
        need_ctx = i < DEPTH - 1
        shift, scale, gate = jnp.split(cond @ ada_w[i] + ada_b[i], 3, axis=-1)
        h = rmsnorm(x, norm_g[i]) * (1.0 + scale) + shift
        hc = None
        if need_ctx or kind != 2:
            shift_c, scale_c, gate_c = jnp.split(cond_ctx @ ada_w[i] + ada_b[i], 3, axis=-1)
            hc = rmsnorm(ctx, norm_g[i]) * (1.0 + scale_c) + shift_c
        if kind == 0:
            o, oc = swa_mixer(h, hc, swa_w_in[j], swa_q_g[j], swa_k_g[j], swa_sink[j], swa_w_out[j], need_ctx)
        elif kind == 1:
            o, oc = mla_mixer(h, hc, mla_w_in[j], mla_qa_g[j], mla_kva_g[j], mla_w_qb[j], mla_w_kvb[j],
                              mla_qn_nope_g[j], mla_qn_pe_g[j], mla_kn_nope_g[j], mla_kn_pe_g[j],
                              mla_w_out[j], need_ctx)
        elif kind == 2:
            hy = (hyena_w_in[j], hyena_conv_w[j], hyena_conv_b[j], hyena_f_w1[j], hyena_f_b1[j],
                  hyena_f_w2[j], hyena_f_b2[j], hyena_f_freq[j], hyena_f_w3[j], hyena_skip[j], hyena_w_out[j])
            o = hyena_branch(h, *hy)
            oc = hyena_branch(hc, *hy) if need_ctx else None
        else:
            lam_init = 0.8 - 0.6 * math.exp(-0.3 * i)
            o, oc = diff_mixer(h, hc, diff_w_in[j], diff_q_g[j], diff_k_g[j], diff_lq1[j], diff_lk1[j],
                               diff_lq2[j], diff_lk2[j], diff_subln_g[j], diff_w_out[j], lam_init, need_ctx)
        x = x + gate * o
        if need_ctx:
            ctx = ctx + gate_c * oc
    return x
```

```python
import functools
import math

import numpy as np
import jax
import jax.numpy as jnp
from jax import lax
from jax.experimental import pallas as pl
from jax.experimental.pallas import tpu as pltpu

F32 = jnp.float32
BF16 = jnp.bfloat16

EPS = 1e-6
NEG_INF = -1e30
LOG2E = 1.4426950408889634
LANES = 128
VMEM_LIMIT_BYTES = 56 * 1024 * 1024

GRID_W = 64
ROPE_BASE = 10000.0
HEAD_DIM = 128
SWA_WINDOW = 128
MLA_Q_RANK, MLA_KV_RANK, MLA_NOPE, MLA_ROPE = 512, 256, 128, 64
HYENA_BANDS = 16
HYENA_DECAY_TARGET, HYENA_FAST_DECAY, HYENA_SLOW_DECAY = 1e-2, 0.3, 1.5
DFT_R = 128


def _params(*sem):
    return pltpu.CompilerParams(dimension_semantics=sem, vmem_limit_bytes=VMEM_LIMIT_BYTES)


def _row(v):
    return v.reshape(1, -1).astype(F32)


def _bf(w):
    return w.astype(BF16)


def _const_bf(m):
    return jnp.asarray(m, F32).astype(BF16)


def _adaln_kernel(c_ref, w_ref, b_ref, o_ref):
    cv = c_ref[...]
    cv = cv * jax.nn.sigmoid(cv)
    w = w_ref[0]
    r0 = jnp.sum(w * cv[:, 0:1], axis=0, keepdims=True)
    r1 = jnp.sum(w * cv[:, 1:2], axis=0, keepdims=True)
    o_ref[0] = jnp.concatenate([r0, r1], axis=0) + b_ref[0]


def _adaln(c, c_ctx, ada_w, ada_b):
    depth, d, n3 = ada_w.shape
    tn = 512
    cc = jnp.stack([c.reshape(d), c_ctx.reshape(d)], axis=1).astype(F32)
    return pl.pallas_call(
        _adaln_kernel,
        out_shape=jax.ShapeDtypeStruct((depth, 2, n3), F32),
        grid=(depth, n3 // tn),
        in_specs=[pl.BlockSpec((d, 2), lambda l, j: (0, 0)),
                  pl.BlockSpec((1, d, tn), lambda l, j: (l, 0, j)),
                  pl.BlockSpec((1, 1, tn), lambda l, j: (l, 0, j))],
        out_specs=pl.BlockSpec((1, 2, tn), lambda l, j: (l, 0, j)),
        compiler_params=_params("parallel", "parallel"),
    )(cc, ada_w, ada_b.reshape(depth, 1, n3))


def _modnorm_kernel(x_ref, g_ref, sc_ref, sh_ref, o_ref):
    x = x_ref[...]
    y = x * lax.rsqrt(jnp.mean(x * x, axis=-1, keepdims=True) + EPS) * g_ref[...]
    o_ref[...] = (y * (1.0 + sc_ref[...]) + sh_ref[...]).astype(o_ref.dtype)


def _modnorm(x, g, scale, shift):
    m, d = x.shape
    tm = min(m, 512)
    vec = pl.BlockSpec((1, d), lambda i: (0, 0))
    return pl.pallas_call(
        _modnorm_kernel,
        out_shape=jax.ShapeDtypeStruct((m, d), BF16),
        grid=(m // tm,),
        in_specs=[pl.BlockSpec((tm, d), lambda i: (i, 0)), vec, vec, vec],
        out_specs=pl.BlockSpec((tm, d), lambda i: (i, 0)),
        compiler_params=_params("parallel"),
    )(x, _row(g), _row(scale), _row(shift))


def _rms(y, g, n_real):
    ss = jnp.sum(y * y, axis=-1, keepdims=True) * (1.0 / n_real)
    return y * lax.rsqrt(ss + EPS) * g


def _rope(y, cos, sin_signed, shift):
    up = pltpu.roll(y, LANES - shift, 1)
    dn = pltpu.roll(y, shift, 1)
    lane = lax.broadcasted_iota(jnp.int32, y.shape, 1)
    rot = jnp.where((lane & (2 * shift - 1)) < shift, up, dn)
    return y * cos + rot * sin_signed


def _epi_plain(acc, o_ref):
    o_ref[...] = acc.astype(o_ref.dtype)


def _epi_silu(acc, o_ref):
    o_ref[...] = (acc * jax.nn.sigmoid(acc)).astype(o_ref.dtype)


def _epi_headnorm(acc, g_ref, o_ref):
    for h in range(acc.shape[1] // LANES):
        sl = slice(h * LANES, (h + 1) * LANES)
        o_ref[:, sl] = _rms(acc[:, sl], g_ref[...], LANES).astype(o_ref.dtype)


def _epi_headnorm_rope(acc, g_ref, cos_ref, sin_ref, o_ref):
    for h in range(acc.shape[1] // LANES):
        sl = slice(h * LANES, (h + 1) * LANES)
        y = _rms(acc[:, sl], g_ref[...], LANES)
        o_ref[:, sl] = _rope(y, cos_ref[...], sin_ref[...], HEAD_DIM // 4).astype(o_ref.dtype)


def _epi_rownorm(acc, g_ref, o_ref):
    o_ref[...] = _rms(acc, g_ref[...], acc.shape[1]).astype(o_ref.dtype)


def _epi_mla_pe(acc, g_ref, cos_ref, sin_ref, o_ref):
    y = _rms(acc, g_ref[...], MLA_ROPE)
    o_ref[...] = _rope(y, cos_ref[...], sin_ref[...], MLA_ROPE // 4).astype(o_ref.dtype)


def _epi_mla_q(acc, gn_ref, gp_ref, cos_ref, sin_ref, o_ref):
    for h in range(acc.shape[1] // (2 * LANES)):
        a = slice(2 * h * LANES, (2 * h + 1) * LANES)
        b = slice((2 * h + 1) * LANES, (2 * h + 2) * LANES)
        o_ref[:, a] = _rms(acc[:, a], gn_ref[...], MLA_NOPE).astype(o_ref.dtype)
        y = _rms(acc[:, b], gp_ref[...], MLA_ROPE)
        o_ref[:, b] = _rope(y, cos_ref[...], sin_ref[...], MLA_ROPE // 4).astype(o_ref.dtype)


def _epi_resid(acc, x_ref, gate_ref, o_ref):
    o_ref[...] = x_ref[...] + gate_ref[...] * acc


def _mm_kernel(*refs, epi):
    a_ref, w_ref = refs[0], refs[1]
    acc = jnp.dot(a_ref[...], w_ref[...], preferred_element_type=F32)
    epi(acc, *refs[2:])


def _mm(a, w, epi, extras=(), out_dtype=BF16, tn=512):
    m, k = a.shape
    n = w.shape[1]
    tm = min(m, 1024)
    tn = min(tn, n)
    assert m % tm == 0 and n % tn == 0
    specs = [pl.BlockSpec((tm, k), lambda i, j: (i, 0)), pl.BlockSpec((k, tn), lambda i, j: (0, j))]
    arrs = [a, w]
    for arr, kind in extras:
        if kind == "row":
            specs.append(pl.BlockSpec((1, tn), lambda i, j: (0, j)))
        elif kind == "tile":
            specs.append(pl.BlockSpec((tm, tn), lambda i, j: (i, j)))
        elif kind == "rows":
            specs.append(pl.BlockSpec((tm, arr.shape[1]), lambda i, j: (i, 0)))
        else:
            specs.append(pl.BlockSpec(arr.shape, lambda i, j: (0, 0)))
        arrs.append(arr)
    return pl.pallas_call(
        functools.partial(_mm_kernel, epi=epi),
        out_shape=jax.ShapeDtypeStruct((m, n), out_dtype),
        grid=(m // tm, n // tn),
        in_specs=specs,
        out_specs=pl.BlockSpec((tm, tn), lambda i, j: (i, j)),
        compiler_params=_params("parallel", "parallel"),
    )(*arrs)


def _rope_tables(n_tok, rot_dim, pad_to, identity_rows):
    half = rot_dim // 2
    qr = half // 2
    rows = n_tok // GRID_W
    row = np.repeat(np.arange(rows, dtype=np.float64), GRID_W)
    col = np.tile(np.arange(GRID_W, dtype=np.float64), rows)
    inv = 1.0 / (ROPE_BASE ** (np.arange(0, half, 2, dtype=np.float64) / half))
    ar = row[:, None] * inv[None, :]
    ac = col[:, None] * inv[None, :]
    ang = np.concatenate([ar, ar, ac, ac], axis=-1)
    sign = np.where((np.arange(rot_dim) % (2 * qr)) < qr, -1.0, 1.0)
    cos = np.cos(ang)
    sin = np.sin(ang) * sign[None, :]
    cos = np.concatenate([cos, np.ones((identity_rows, rot_dim))], axis=0)
    sin = np.concatenate([sin, np.zeros((identity_rows, rot_dim))], axis=0)
    pad = ((0, 0), (0, pad_to - rot_dim))
    return (jnp.asarray(np.pad(cos, pad), F32), jnp.asarray(np.pad(sin, pad), F32))


class _AttnCfg:
    def __init__(self, maps, dqk, dv, scale, n_src, kx, sink, window, diff_lam_init, tq, tk):
        self.maps, self.dqk, self.dv, self.scale = maps, dqk, dv, scale
        self.n_src, self.kx, self.sink, self.window = n_src, kx, sink, window
        self.diff_lam_init, self.tq, self.tk = diff_lam_init, tq, tk


def _attn_kernel(*refs, cfg):
    it = iter(refs)
    q_ref = next(it)
    srcs = []
    for _ in range(cfg.n_src):
        k_ref = next(it)
        kx_ref = next(it) if cfg.kx else None
        v_ref = next(it)
        srcs.append((k_ref, kx_ref, v_ref))
    sink_ref = next(it) if cfg.sink else None
    gate_ref = next(it)
    if cfg.diff_lam_init is not None:
        lam_ref, subg_ref = next(it), next(it)
    o_ref = next(it)
    m_ref, l_ref, acc_ref = next(it), next(it), next(it)

    tq, tk, maps, dqk = cfg.tq, cfg.tk, cfg.maps, cfg.dqk
    qi = pl.program_id(1)
    c = cfg.scale * LOG2E

    if cfg.sink:
        m_ref[...] = jnp.full(m_ref.shape, 1.0, F32) * (sink_ref[0, 0:1, 0:1] * LOG2E)
        l_ref[...] = jnp.ones(l_ref.shape, F32)
    else:
        m_ref[...] = jnp.full(m_ref.shape, NEG_INF, F32)
        l_ref[...] = jnp.zeros(l_ref.shape, F32)
    acc_ref[...] = jnp.zeros(acc_ref.shape, F32)

    def make_body(k_ref, kx_ref, v_ref, masked):
        def body(kt, carry):
            koff = pl.multiple_of(kt * tk, tk)
            v = v_ref[pl.ds(koff, tk), :]
            for mp in range(maps):
                q = q_ref[:, mp * dqk:(mp + 1) * dqk]
                if kx_ref is None:
                    k = k_ref[pl.ds(koff, tk), mp * dqk:(mp + 1) * dqk]
                else:
                    k = jnp.concatenate([k_ref[pl.ds(koff, tk), :], kx_ref[pl.ds(koff, tk), :]], axis=1)
                s = lax.dot_general(q, k, (((1,), (1,)), ((), ())), preferred_element_type=F32) * c
                if masked:
                    qpos = qi * tq + lax.broadcasted_iota(jnp.int32, (tq, tk), 0)
                    kpos = koff + lax.broadcasted_iota(jnp.int32, (tq, tk), 1)
                    dlt = kpos - qpos
                    s = jnp.where((dlt >= -cfg.window) & (dlt <= cfg.window), s, NEG_INF)
                m_prev = m_ref[mp]
                m_new = jnp.maximum(m_prev, jnp.max(s, axis=-1, keepdims=True))
                alpha = jnp.exp2(m_prev - m_new)
                p = jnp.exp2(s - m_new)
                l_ref[mp] = alpha * l_ref[mp] + jnp.sum(p, axis=-1, keepdims=True)
                acc_ref[mp] = alpha * acc_ref[mp] + jnp.dot(p.astype(BF16), v, preferred_element_type=F32)
                m_ref[mp] = m_new
            return carry
        return body

    for si, (k_ref, kx_ref, v_ref) in enumerate(srcs):
        nk = k_ref.shape[0] // tk
        if cfg.window is not None and si == 0:
            first = qi * (tq // tk) - 1
            lo = jnp.maximum(first, 0)
            hi = jnp.minimum(first + tq // tk + 2, nk)
            lax.fori_loop(lo, hi, make_body(k_ref, kx_ref, v_ref, True), 0)
        else:
            lax.fori_loop(0, nk, make_body(k_ref, kx_ref, v_ref, False), 0)

    gate = gate_ref[...].astype(F32)
    if cfg.diff_lam_init is None:
        o = acc_ref[0] / l_ref[0]
    else:
        lam = (jnp.exp(jnp.sum(lam_ref[0:1, :] * lam_ref[1:2, :], axis=-1, keepdims=True))
               - jnp.exp(jnp.sum(lam_ref[2:3, :] * lam_ref[3:4, :], axis=-1, keepdims=True))
               + cfg.diff_lam_init)
        o = acc_ref[0] / l_ref[0] - lam * (acc_ref[1] / l_ref[1])
        o = _rms(o, subg_ref[...], cfg.dv) * (1.0 - cfg.diff_lam_init)
    o_ref[...] = (o * gate).astype(o_ref.dtype)


def _attention(q, srcs, gate, *, heads, kv_group, maps, dqk, dv, scale, sink=None, window=None,
               diff=None):
    mq = q.shape[0]
    tq = min(mq, 512)
    tk = 256
    kx = srcs[0][1] is not None
    cfg = _AttnCfg(maps, dqk, dv, scale, len(srcs), kx, sink is not None, window,
                   None if diff is None else diff[0], tq, tk)
    kw = maps * dqk - (LANES if kx else 0)
    specs = [pl.BlockSpec((tq, maps * dqk), lambda h, i: (i, h))]
    arrs = [q]
    for k, kxa, v in srcs:
        nk = k.shape[0]
        assert nk % tk == 0
        specs.append(pl.BlockSpec((nk, kw), lambda h, i: (0, h // kv_group)))
        arrs.append(k)
        if kx:
            specs.append(pl.BlockSpec((nk, LANES), lambda h, i: (0, 0)))
            arrs.append(kxa)
        specs.append(pl.BlockSpec((nk, dv), lambda h, i: (0, h // kv_group)))
        arrs.append(v)
    if sink is not None:
        specs.append(pl.BlockSpec((1, 8, LANES), lambda h, i: (h, 0, 0)))
        arrs.append(jnp.broadcast_to(sink.astype(F32)[:, None, None], (heads, 8, LANES)))
    specs.append(pl.BlockSpec((tq, dv), lambda h, i: (i, h)))
    arrs.append(gate)
    if diff is not None:
        specs.append(pl.BlockSpec((4, LANES), lambda h, i: (0, 0)))
        arrs.append(diff[1])
        specs.append(pl.BlockSpec((1, dv), lambda h, i: (0, 0)))
        arrs.append(diff[2])
    return pl.pallas_call(
        functools.partial(_attn_kernel, cfg=cfg),
        out_shape=jax.ShapeDtypeStruct((mq, heads * dv), BF16),
        grid=(heads, mq // tq),
        in_specs=specs,
        out_specs=pl.BlockSpec((tq, dv), lambda h, i: (i, h)),
        scratch_shapes=[pltpu.VMEM((maps, tq, 1), F32), pltpu.VMEM((maps, tq, 1), F32),
                        pltpu.VMEM((maps, tq, dv), F32)],
        compiler_params=_params("parallel", "parallel"),
    )(*arrs)


def _conv3_kernel(u_ref, prev_ref, next_ref, w_ref, b_ref, o_ref):
    i = pl.program_id(0)
    tm = u_ref.shape[0]
    x = u_ref[...]
    row = lax.broadcasted_iota(jnp.int32, x.shape, 0)
    prev_row = jnp.where(i > 0, prev_ref[7:8, :], 0.0)
    next_row = jnp.where(i < pl.num_programs(0) - 1, next_ref[0:1, :], 0.0)
    xm = jnp.where(row == 0, prev_row, pltpu.roll(x, 1, 0))
    xp = jnp.where(row == tm - 1, next_row, pltpu.roll(x, tm - 1, 0))
    o_ref[...] = xm * w_ref[0:1, :] + x * w_ref[1:2, :] + xp * w_ref[2:3, :] + b_ref[...]


def _conv3(u, w, b):
    m, n = u.shape
    tm = min(m, 512)
    tn = 512
    nrb = m // 8
    return pl.pallas_call(
        _conv3_kernel,
        out_shape=jax.ShapeDtypeStruct((m, n), F32),
        grid=(m // tm, n // tn),
        in_specs=[pl.BlockSpec((tm, tn), lambda i, j: (i, j)),
                  pl.BlockSpec((8, tn), lambda i, j: (jnp.maximum(i * (tm // 8) - 1, 0), j)),
                  pl.BlockSpec((8, tn), lambda i, j: (jnp.minimum((i + 1) * (tm // 8), nrb - 1), j)),
                  pl.BlockSpec((3, tn), lambda i, j: (0, j)),
                  pl.BlockSpec((1, tn), lambda i, j: (0, j))],
        out_specs=pl.BlockSpec((tm, tn), lambda i, j: (i, j)),
        compiler_params=_params("parallel", "parallel"),
    )(u, u, u, w.astype(F32), _row(b))


def _split_dot(a, b):
    ah = a.astype(BF16)
    al = (a - ah.astype(F32)).astype(BF16)
    bh = b.astype(BF16)
    bl = (b - bh.astype(F32)).astype(BF16)
    d = functools.partial(jnp.dot, preferred_element_type=F32)
    return d(ah, bh) + (d(ah, bl) + d(al, bh))


def _filter_kernel(z_ref, ts_ref, w1_ref, b1_ref, w2_ref, b2_ref, fr_ref, w3_ref, dl_ref, o_ref, ss_ref):
    h = jnp.sin(fr_ref[0:1, :] * (_split_dot(z_ref[...], w1_ref[...]) + b1_ref[...]))
    h = jnp.sin(fr_ref[1:2, :] * (_split_dot(h, w2_ref[...]) + b2_ref[...]))
    hf = _split_dot(h, w3_ref[0])
    kc = hf * jnp.exp(-ts_ref[:, 0:1] * dl_ref[...]) * ts_ref[:, 1:2]
    o_ref[...] = kc

    @pl.when(pl.program_id(0) == 0)
    def _():
        ss_ref[...] = jnp.zeros(ss_ref.shape, F32)

    ss_ref[...] += jnp.sum(kc * kc, axis=0, keepdims=True)


def _filter_tables(L):
    pos = np.arange(L, dtype=np.float32)
    t = (pos / np.float32(max(L - 1, 1))).astype(np.float32)
    bands = np.linspace(1e-4, HYENA_BANDS - 1, HYENA_BANDS, dtype=np.float32)
    ang = (np.float32(2.0 * math.pi / L) * pos[:, None] * bands[None, :]).astype(np.float32)
    z = np.concatenate([t[:, None], np.cos(ang), -np.sin(ang)], axis=-1).astype(np.float32)
    src = np.concatenate([np.arange(L), np.array([0]), np.arange(L - 1, 0, -1)])
    sign = np.concatenate([np.ones(L), np.zeros(1), -np.ones(L - 1)])
    zt = np.zeros((2 * L, LANES), np.float32)
    zt[:, :z.shape[1]] = z[src]
    ts = np.stack([t[src], sign.astype(np.float32)], axis=1)
    return jnp.asarray(zt), jnp.asarray(ts)


def _hyena_filter(L, f_w1, f_b1, f_w2, f_b2, f_freq, f_w3, width):
    zt, ts = _filter_tables(L)
    hid = f_w1.shape[1]
    w1 = jnp.zeros((LANES, hid), F32).at[:f_w1.shape[0]].set(f_w1.astype(F32))
    w3 = f_w3.astype(F32).reshape(hid, 2, 2, width)
    w3 = jnp.stack([w3[:, :, 0].reshape(hid, 2 * width), w3[:, :, 1].reshape(hid, 2 * width)])
    max_decay = math.log(HYENA_DECAY_TARGET) / HYENA_FAST_DECAY
    min_decay = math.log(HYENA_DECAY_TARGET) / HYENA_SLOW_DECAY
    deltas = np.abs(np.linspace(min_decay, max_decay, width, dtype=np.float32))
    dl = jnp.asarray(np.tile(deltas, 2)[None, :], F32)
    tr = min(L, 512)
    nfw = L // tr
    cw = 2 * width
    full = lambda shp: pl.BlockSpec(shp, lambda i: (0,) * len(shp))
    return pl.pallas_call(
        _filter_kernel,
        out_shape=(jax.ShapeDtypeStruct((2 * L, cw), F32), jax.ShapeDtypeStruct((1, cw), F32)),
        grid=(2 * L // tr,),
        in_specs=[pl.BlockSpec((tr, LANES), lambda i: (i, 0)), pl.BlockSpec((tr, 2), lambda i: (i, 0)),
                  full((LANES, hid)), full((1, hid)), full((hid, hid)), full((1, hid)), full((2, hid)),
                  pl.BlockSpec((1, hid, cw), lambda i: (i // nfw, 0, 0)), full((1, cw))],
        out_specs=(pl.BlockSpec((tr, cw), lambda i: (i, 0)), full((1, cw))),
        compiler_params=_params("arbitrary"),
    )(zt, ts, w1, _row(f_b1), f_w2.astype(F32), _row(f_b2), f_freq.astype(F32), w3, dl)


@functools.lru_cache(maxsize=None)
def _dft_mats():
    r = DFT_R
    n = r * r
    k1 = np.arange(r)[:, None]
    n1 = np.arange(r)[None, :]
    base = np.exp(-2j * np.pi * n1 * (k1 + 0.5) / r)
    n2 = np.arange(r)[:, None, None]
    tw = np.exp(-2j * np.pi * n2 * (k1[None] + 0.5) / n)
    ma = tw * base[None]
    ma = np.concatenate([ma.real, ma.imag], axis=1)
    k2 = np.arange(r // 2)[:, None]
    fb = np.exp(-2j * np.pi * np.arange(r)[None, :] * k2 / r)
    fb = np.block([[fb.real, -fb.imag], [fb.imag, fb.real]])
    mc = np.conj(fb[: r // 2, :r] + 1j * fb[r // 2:, :r]).T
    mc = np.block([[mc.real, -mc.imag], [mc.imag, mc.real]])
    md = np.conj(tw * base[None]).transpose(0, 2, 1)[:, : r // 2, :] * (2.0 / n)
    md = np.concatenate([md.real, -md.imag], axis=2)
    return tuple(np.asarray(m, np.float32) for m in (ma, fb, mc, md))


def _dft_a_kernel(m_ref, x_ref, o_ref, *, nb, ch):
    for j in range(nb):
        xs = x_ref[:, j * ch:(j + 1) * ch].astype(BF16)
        o_ref[:, j * ch:(j + 1) * ch] = jnp.dot(m_ref[j], xs, preferred_element_type=F32).astype(BF16)


def _dft_stage_a(x2, ch):
    r = DFT_R
    kdim = x2.shape[0]
    ma = _const_bf(_dft_mats()[0][:, :, :kdim])
    nb = max(1, min(r, 16384 // ch))
    return pl.pallas_call(
        functools.partial(_dft_a_kernel, nb=nb, ch=ch),
        out_shape=jax.ShapeDtypeStruct((2 * r, r * ch), BF16),
        grid=(r // nb,),
        in_specs=[pl.BlockSpec((nb, 2 * r, kdim), lambda i: (i, 0, 0)),
                  pl.BlockSpec((kdim, nb * ch), lambda i: (0, i))],
        out_specs=pl.BlockSpec((2 * r, nb * ch), lambda i: (0, i)),
        compiler_params=_params("parallel"),
    )(ma, x2)


def _dft_b_kernel(fb_ref, a_ref, ss_ref, o_ref, *, kb):
    rn = lax.rsqrt(ss_ref[...] + EPS)
    for j in range(kb):
        op = jnp.concatenate([a_ref[0, j], a_ref[1, j]], axis=0)
        o_ref[j] = jnp.dot(fb_ref[...], op, preferred_element_type=F32) * rn


def _dft_stage_b_spectrum(a2, ss, ch):
    r = DFT_R
    fb = _const_bf(_dft_mats()[1])
    kb = max(1, min(r, 8192 // ch))
    a4 = a2.reshape(2, r, r, ch)
    return pl.pallas_call(
        functools.partial(_dft_b_kernel, kb=kb),
        out_shape=jax.ShapeDtypeStruct((r, r, ch), F32),
        grid=(r // kb,),
        in_specs=[pl.BlockSpec((r, 2 * r), lambda i: (0, 0)),
                  pl.BlockSpec((2, kb, r, ch), lambda i: (0, i, 0, 0)),
                  pl.BlockSpec((1, ch), lambda i: (0, 0))],
        out_specs=pl.BlockSpec((kb, r, ch), lambda i: (i, 0, 0)),
        compiler_params=_params("parallel"),
    )(fb, a4, ss)


def _cmul(x, s):
    h = x.shape[0] // 2
    xr, xi, sr, si = x[:h], x[h:], s[:h], s[h:]
    return jnp.concatenate([xr * sr - xi * si, xr * si + xi * sr], axis=0)


def _dft_bc_kernel(fb_ref, mc_ref, a_ref, s_ref, o_ref, *, kb, ch):
    for j in range(kb):
        op = jnp.concatenate([a_ref[0, j], a_ref[1, j]], axis=0)
        xs = jnp.dot(fb_ref[...], op, preferred_element_type=F32)
        y = _cmul(xs, s_ref[j]).astype(BF16)
        o_ref[:, j * ch:(j + 1) * ch] = jnp.dot(mc_ref[...], y, preferred_element_type=F32).astype(BF16)


def _dft_stage_bc(a2, spec, order, ch):
    r = DFT_R
    fb, mc = (_const_bf(m) for m in _dft_mats()[1:3])
    kb = max(1, min(r, 8192 // ch))
    a4 = a2.reshape(2, r, r, ch)
    return pl.pallas_call(
        functools.partial(_dft_bc_kernel, kb=kb, ch=ch),
        out_shape=jax.ShapeDtypeStruct((2 * r, r * ch), BF16),
        grid=(r // kb,),
        in_specs=[pl.BlockSpec((r, 2 * r), lambda i: (0, 0)),
                  pl.BlockSpec((2 * r, r), lambda i: (0, 0)),
                  pl.BlockSpec((2, kb, r, ch), lambda i: (0, i, 0, 0)),
                  pl.BlockSpec((kb, r, ch), lambda i: (i, 0, order))],
        out_specs=pl.BlockSpec((2 * r, kb * ch), lambda i: (0, i)),
        compiler_params=_params("parallel"),
    )(fb, mc, a4, spec)


def _dft_d_kernel(*refs, nb, ch, gated):
    md_ref, c_ref, u_ref, mul_ref, skip_ref = refs[:5]
    gate_ref = refs[5] if gated else None
    o_ref = refs[-1]
    for j in range(nb):
        sl = slice(j * ch, (j + 1) * ch)
        op = jnp.concatenate([c_ref[0, j], c_ref[1, j]], axis=0)
        y = jnp.dot(md_ref[j], op, preferred_element_type=F32)
        z = mul_ref[:, sl] * (y + u_ref[:, sl] * skip_ref[...])
        if gated:
            z = z * gate_ref[:, sl].astype(F32)
        o_ref[:, sl] = z.astype(o_ref.dtype)


def _dft_stage_d(c2, u2, mul2, skip, gate2, ch):
    r = DFT_R
    md = _const_bf(_dft_mats()[3])
    nb = max(1, min(r, 8192 // ch))
    c4 = c2.reshape(2, r, r, ch)
    blk = pl.BlockSpec((r // 2, nb * ch), lambda i: (0, i))
    specs = [pl.BlockSpec((nb, r // 2, 2 * r), lambda i: (i, 0, 0)),
             pl.BlockSpec((2, nb, r, ch), lambda i: (0, i, 0, 0)),
             blk, blk, pl.BlockSpec((1, ch), lambda i: (0, 0))]
    arrs = [md, c4, u2, mul2, skip]
    if gate2 is not None:
        specs.append(blk)
        arrs.append(gate2)
    return pl.pallas_call(
        functools.partial(_dft_d_kernel, nb=nb, ch=ch, gated=gate2 is not None),
        out_shape=jax.ShapeDtypeStruct((r // 2, r * ch), F32 if gate2 is None else BF16),
        grid=(r // nb,),
        in_specs=specs,
        out_specs=blk,
        compiler_params=_params("parallel"),
    )(*arrs)


def _hyena_long(u, gate, filt, ss, skip, width):
    L = u.shape[0]
    r = DFT_R
    assert 2 * L == r * r
    cw = 2 * width
    spec = _dft_stage_b_spectrum(_dft_stage_a(filt.reshape(r, r * cw), cw), ss, cw)
    lay = lambda t: t.reshape(r // 2, r * width)
    v, x1, x2 = (lay(u[:, k * width:(k + 1) * width]) for k in range(3))
    sk = skip.astype(F32)
    c2 = _dft_stage_bc(_dft_stage_a(v, width), spec, 0, width)
    z = _dft_stage_d(c2, v, x1, sk[0:1], None, width)
    c2 = _dft_stage_bc(_dft_stage_a(z, width), spec, 1, width)
    z = _dft_stage_d(c2, z, x2, sk[1:2], lay(gate), width)
    return z.reshape(L, width)


@functools.lru_cache(maxsize=None)
def _short_dft_mats(L):
    n = 2 * L
    k = np.arange(L)[:, None]
    t = np.arange(n)[None, :]
    f = np.exp(-2j * np.pi * t * (k + 0.5) / n)
    fwd = np.concatenate([f.real, f.imag], axis=0)
    inv = np.conj(f[:, :L]).T * (2.0 / n)
    inv = np.concatenate([inv.real, -inv.imag], axis=1)
    return np.asarray(fwd, np.float32), np.asarray(inv, np.float32)


def _short_spec_kernel(f_ref, k_ref, ss_ref, o_ref):
    o_ref[...] = jnp.dot(f_ref[...], k_ref[...].astype(BF16), preferred_element_type=F32) * lax.rsqrt(ss_ref[...] + EPS)


def _short_conv_kernel(*refs, gated):
    f_ref, i_ref, u_ref, s_ref, mul_ref, skip_ref = refs[:6]
    gate_ref = refs[6] if gated else None
    o_ref = refs[-1]
    u = u_ref[...]
    xs = jnp.dot(f_ref[...], u.astype(BF16), preferred_element_type=F32)
    y = jnp.dot(i_ref[...], _cmul(xs, s_ref[...]).astype(BF16), preferred_element_type=F32)
    z = mul_ref[...] * (y + u * skip_ref[...])
    if gated:
        z = z * gate_ref[...].astype(F32)
    o_ref[...] = z.astype(o_ref.dtype)


def _hyena_short(u, gate, filt, ss, skip, width):
    L = u.shape[0]
    fwd, inv = (_const_bf(m) for m in _short_dft_mats(L))
    cw = 2 * width
    tn = 512
    spec = pl.pallas_call(
        _short_spec_kernel,
        out_shape=jax.ShapeDtypeStruct((2 * L, cw), F32),
        grid=(cw // tn,),
        in_specs=[pl.BlockSpec((2 * L, 2 * L), lambda j: (0, 0)), pl.BlockSpec((2 * L, tn), lambda j: (0, j)),
                  pl.BlockSpec((1, tn), lambda j: (0, j))],
        out_specs=pl.BlockSpec((2 * L, tn), lambda j: (0, j)),
        compiler_params=_params("parallel"),
    )(fwd, filt, ss)

    def conv(uarr, ucol, order, mulcol, gate_arr):
        nt = width // tn
        specs = [pl.BlockSpec((2 * L, L), lambda j: (0, 0)), pl.BlockSpec((L, 2 * L), lambda j: (0, 0)),
                 pl.BlockSpec((L, tn), lambda j: (0, ucol * nt + j)),
                 pl.BlockSpec((2 * L, tn), lambda j: (0, order * nt + j)),
                 pl.BlockSpec((L, tn), lambda j: (0, mulcol * nt + j)),
                 pl.BlockSpec((1, tn), lambda j: (0, order * nt + j))]
        arrs = [fwd[:, :L], inv, uarr, spec, u, skip.astype(F32).reshape(1, cw)]
        if gate_arr is not None:
            specs.append(pl.BlockSpec((L, tn), lambda j: (0, j)))
            arrs.append(gate_arr)
        return pl.pallas_call(
            functools.partial(_short_conv_kernel, gated=gate_arr is not None),
            out_shape=jax.ShapeDtypeStruct((L, width), F32 if gate_arr is None else BF16),
            grid=(nt,),
            in_specs=specs,
            out_specs=pl.BlockSpec((L, tn), lambda j: (0, j)),
            compiler_params=_params("parallel"),
        )(*arrs)

    z = conv(u, 0, 0, 1, None)
    return conv(z, 0, 1, 2, gate)


def _swa_layer(h, hc, w_in, q_g, k_g, sink, need_ctx):
    n, c = h.shape[0], hc.shape[0]
    hq, hk, d = 16, 4, HEAD_DIM
    wq, wk, wv, wg = (_bf(w_in[:, a:b]) for a, b in
                      ((0, hq * d), (hq * d, (hq + hk) * d), ((hq + hk) * d, (hq + 2 * hk) * d),
                       ((hq + 2 * hk) * d, 2 * hq * d + 2 * hk * d)))
    cos, sin = _rope_tables(n, d, d, 0)
    qg, kg = _row(q_g), _row(k_g)
    rope = lambda g: [(g, "const"), (cos, "rows"), (sin, "rows")]
    q = _mm(h, wq, _epi_headnorm_rope, rope(qg))
    k = _mm(h, wk, _epi_headnorm_rope, rope(kg))
    v = _mm(h, wv, _epi_plain)
    g = _mm(h, wg, _epi_silu)
    ck = _mm(hc, wk, _epi_headnorm, [(kg, "const")])
    cv = _mm(hc, wv, _epi_plain)
    kw = dict(heads=hq, kv_group=hq // hk, maps=1, dqk=d, dv=d, scale=d ** -0.5, sink=sink)
    o = _attention(q, [(k, None, v), (ck, None, cv)], g, window=SWA_WINDOW, **kw)
    oc = None
    if need_ctx:
        cq = _mm(hc, wq, _epi_headnorm, [(qg, "const")])
        cg = _mm(hc, wg, _epi_silu)
        oc = _attention(cq, [(ck, None, cv)], cg, **kw)
    return o, oc


def _mla_layer(h, hc, w_in, qa_g, kva_g, w_qb, w_kvb, qn_nope_g, qn_pe_g, kn_nope_g, kn_pe_g, need_ctx):
    n, c = h.shape[0], hc.shape[0]
    nh = 16
    qr, kvr, rp, nope = MLA_Q_RANK, MLA_KV_RANK, MLA_ROPE, MLA_NOPE
    w_cq = _bf(w_in[:, :qr])
    w_ckv = _bf(w_in[:, qr:qr + kvr])
    w_kr = _bf(jnp.pad(w_in[:, qr + kvr:qr + kvr + rp], ((0, 0), (0, LANES - rp))))
    w_g = _bf(w_in[:, qr + kvr + rp:])
    wqb = jnp.pad(w_qb.reshape(qr, nh, nope + rp), ((0, 0), (0, 0), (0, 2 * LANES - nope - rp)))
    wqb = _bf(wqb.reshape(qr, nh * 2 * LANES))
    wkv = w_kvb.reshape(kvr, nh, 2 * LANES)
    w_kn, w_v = _bf(wkv[:, :, :nope].reshape(kvr, nh * nope)), _bf(wkv[:, :, nope:].reshape(kvr, nh * LANES))
    pad64 = lambda g: jnp.pad(_row(g), ((0, 0), (0, LANES - rp)))
    gqp, gkp = pad64(qn_pe_g), pad64(kn_pe_g)
    cos, sin = _rope_tables(n, rp, LANES, 0)
    cos_c = jnp.pad(jnp.ones((c, rp), F32), ((0, 0), (0, LANES - rp)))
    sin_c = jnp.zeros((c, LANES), F32)

    def side(hh, cs, sn, queries):
        ckv = _mm(hh, w_ckv, _epi_rownorm, [(_row(kva_g), "row")], tn=kvr)
        kp = _mm(hh, w_kr, _epi_mla_pe, [(gkp, "const"), (cs, "rows"), (sn, "rows")])
        kn = _mm(ckv, w_kn, _epi_headnorm, [(_row(kn_nope_g), "const")])
        v = _mm(ckv, w_v, _epi_plain)
        if not queries:
            return kn, kp, v, None, None
        cq = _mm(hh, w_cq, _epi_rownorm, [(_row(qa_g), "row")], tn=qr)
        q = _mm(cq, wqb, _epi_mla_q, [(_row(qn_nope_g), "const"), (gqp, "const"), (cs, "rows"), (sn, "rows")])
        g = _mm(hh, w_g, _epi_silu)
        return kn, kp, v, q, g

    kn, kp, v, q, g = side(h, cos, sin, True)
    ckn, ckp, cv, cq, cg = side(hc, cos_c, sin_c, need_ctx)
    kw = dict(heads=nh, kv_group=1, maps=1, dqk=2 * LANES, dv=LANES, scale=(nope + rp) ** -0.5)
    o = _attention(q, [(kn, kp, v), (ckn, ckp, cv)], g, **kw)
    oc = _attention(cq, [(ckn, ckp, cv)], cg, **kw) if need_ctx else None
    return o, oc


def _hyena_layer(h, hc, w_in, conv_w, conv_b, f_w1, f_b1, f_w2, f_b2, f_freq, f_w3, skip, need_ctx):
    width = skip.shape[1]
    w_u, w_g = _bf(w_in[:, :3 * width]), _bf(w_in[:, 3 * width:])

    def branch(hh, long):
        L = hh.shape[0]
        u = _conv3(_mm(hh, w_u, _epi_plain, out_dtype=F32), conv_w, conv_b)
        g = _mm(hh, w_g, _epi_silu)
        filt, ss = _hyena_filter(L, f_w1, f_b1, f_w2, f_b2, f_freq, f_w3, width)
        return (_hyena_long if long else _hyena_short)(u, g, filt, ss, skip, width)

    o = branch(h, True)
    oc = branch(hc, False) if need_ctx else None
    return o, oc


def _diff_layer(h, hc, w_in, q_g, k_g, lq1, lk1, lq2, lk2, subln_g, lam_init, need_ctx):
    n, c = h.shape[0], hc.shape[0]
    nh, d = 8, HEAD_DIM
    w = nh * 2 * d
    wq, wk, wv, wg = (_bf(w_in[:, i * w:(i + 1) * w]) for i in range(4))
    cos, sin = _rope_tables(n, d, d, 0)
    qg, kg = _row(q_g), _row(k_g)
    rope = lambda g: [(g, "const"), (cos, "rows"), (sin, "rows")]
    q = _mm(h, wq, _epi_headnorm_rope, rope(qg))
    k = _mm(h, wk, _epi_headnorm_rope, rope(kg))
    v = _mm(h, wv, _epi_plain)
    g = _mm(h, wg, _epi_silu)
    ck = _mm(hc, wk, _epi_headnorm, [(kg, "const")])
    cv = _mm(hc, wv, _epi_plain)
    lam_vecs = jnp.stack([lq1, lk1, lq2, lk2]).astype(F32)
    kw = dict(heads=nh, kv_group=1, maps=2, dqk=d, dv=2 * d, scale=d ** -0.5,
              diff=(lam_init, lam_vecs, _row(subln_g)))
    o = _attention(q, [(k, None, v), (ck, None, cv)], g, **kw)
    oc = None
    if need_ctx:
        cq = _mm(hc, wq, _epi_headnorm, [(qg, "const")])
        cg = _mm(hc, wg, _epi_silu)
        oc = _attention(cq, [(ck, None, cv)], cg, **kw)
    return o, oc


def kernel(x, c, ctx, c_ctx, norm_g, ada_w, ada_b, swa_w_in, swa_q_g, swa_k_g, swa_sink, swa_w_out, mla_w_in, mla_qa_g, mla_kva_g, mla_w_qb, mla_w_kvb, mla_qn_nope_g, mla_qn_pe_g, mla_kn_nope_g, mla_kn_pe_g, mla_w_out, hyena_w_in, hyena_conv_w, hyena_conv_b, hyena_f_w1, hyena_f_b1, hyena_f_w2, hyena_f_b2, hyena_f_freq, hyena_f_w3, hyena_skip, hyena_w_out, diff_w_in, diff_q_g, diff_k_g, diff_lq1, diff_lk1, diff_lq2, diff_lk2, diff_subln_g, diff_w_out):
    depth, d = norm_g.shape
    assert x.shape[0] == 1
    xs, cs = x[0], ctx[0]
    mods = _adaln(c, c_ctx, ada_w, ada_b)
    for i in range(depth):
        kind, j = i % 4, i // 4
        need_ctx = i < depth - 1
        shift, scale, gate = (mods[i, 0, a * d:(a + 1) * d] for a in range(3))
        shift_c, scale_c, gate_c = (mods[i, 1, a * d:(a + 1) * d] for a in range(3))
        h = _modnorm(xs, norm_g[i], scale, shift)
        hc = _modnorm(cs, norm_g[i], scale_c, shift_c)
        if kind == 0:
            o, oc = _swa_layer(h, hc, swa_w_in[j], swa_q_g[j], swa_k_g[j], swa_sink[j], need_ctx)
            w_out = swa_w_out[j]
        elif kind == 1:
            o, oc = _mla_layer(h, hc, mla_w_in[j], mla_qa_g[j], mla_kva_g[j], mla_w_qb[j], mla_w_kvb[j],
                               mla_qn_nope_g[j], mla_qn_pe_g[j], mla_kn_nope_g[j], mla_kn_pe_g[j], need_ctx)
            w_out = mla_w_out[j]
        elif kind == 2:
            o, oc = _hyena_layer(h, hc, hyena_w_in[j], hyena_conv_w[j], hyena_conv_b[j], hyena_f_w1[j],
                                 hyena_f_b1[j], hyena_f_w2[j], hyena_f_b2[j], hyena_f_freq[j], hyena_f_w3[j],
                                 hyena_skip[j], need_ctx)
            w_out = hyena_w_out[j]
        else:
            lam_init = 0.8 - 0.6 * math.exp(-0.3 * i)
            o, oc = _diff_layer(h, hc, diff_w_in[j], diff_q_g[j], diff_k_g[j], diff_lq1[j], diff_lk1[j],
                                diff_lq2[j], diff_lk2[j], diff_subln_g[j], lam_init, need_ctx)
            w_out = diff_w_out[j]
        wo = _bf(w_out)
        xs = _mm(o, wo, _epi_resid, [(xs, "tile"), (_row(gate), "row")], out_dtype=F32)
        if need_ctx:
            cs = _mm(oc, wo, _epi_resid, [(cs, "tile"), (_row(gate_c), "row")], out_dtype=F32)
    return xs[None]
```

```python
import functools
import math

import numpy as np
import jax
import jax.numpy as jnp
from jax import lax
from jax.experimental import pallas as pl
from jax.experimental.pallas import tpu as pltpu

F32 = jnp.float32
BF16 = jnp.bfloat16

EPS = 1e-6
NEG_INF = -1e30
LOG2E = 1.4426950408889634
LANES = 128
VMEM_LIMIT_BYTES = 56 * 1024 * 1024

GRID_W = 64
ROPE_BASE = 10000.0
HEAD_DIM = 128
SWA_WINDOW = 128
MLA_Q_RANK, MLA_KV_RANK, MLA_NOPE, MLA_ROPE = 512, 256, 128, 64
HYENA_BANDS = 16
HYENA_DECAY_TARGET, HYENA_FAST_DECAY, HYENA_SLOW_DECAY = 1e-2, 0.3, 1.5
DFT_R = 128
ATTN_TK = 256


def _params(*sem):
    return pltpu.CompilerParams(dimension_semantics=sem, vmem_limit_bytes=VMEM_LIMIT_BYTES)


def _row(v):
    return v.reshape(1, -1).astype(F32)


def _bf(w):
    return w.astype(BF16)


def _const_bf(m):
    return jnp.asarray(m, F32).astype(BF16)


def _adaln_kernel(c_ref, w_ref, b_ref, o_ref):
    cv = c_ref[...]
    cv = cv * jax.nn.sigmoid(cv)
    w = w_ref[0]
    r0 = jnp.sum(w * cv[:, 0:1], axis=0, keepdims=True)
    r1 = jnp.sum(w * cv[:, 1:2], axis=0, keepdims=True)
    o_ref[0] = jnp.concatenate([r0, r1], axis=0) + b_ref[0]


def _adaln(c, c_ctx, ada_w, ada_b):
    depth, d, n3 = ada_w.shape
    tn = 512
    cc = jnp.stack([c.reshape(d), c_ctx.reshape(d)], axis=1).astype(F32)
    return pl.pallas_call(
        _adaln_kernel,
        out_shape=jax.ShapeDtypeStruct((depth, 2, n3), F32),
        grid=(depth, n3 // tn),
        in_specs=[pl.BlockSpec((d, 2), lambda l, j: (0, 0)),
                  pl.BlockSpec((1, d, tn), lambda l, j: (l, 0, j)),
                  pl.BlockSpec((1, 1, tn), lambda l, j: (l, 0, j))],
        out_specs=pl.BlockSpec((1, 2, tn), lambda l, j: (l, 0, j)),
        compiler_params=_params("parallel", "parallel"),
    )(cc, ada_w, ada_b.reshape(depth, 1, n3))


def _modnorm_kernel(x_ref, g_ref, sc_ref, sh_ref, o_ref):
    x = x_ref[...]
    y = x * lax.rsqrt(jnp.mean(x * x, axis=-1, keepdims=True) + EPS) * g_ref[...]
    o_ref[...] = (y * (1.0 + sc_ref[...]) + sh_ref[...]).astype(o_ref.dtype)


def _modnorm(x, g, scale, shift):
    m, d = x.shape
    tm = min(m, 512)
    vec = pl.BlockSpec((1, d), lambda i: (0, 0))
    return pl.pallas_call(
        _modnorm_kernel,
        out_shape=jax.ShapeDtypeStruct((m, d), BF16),
        grid=(m // tm,),
        in_specs=[pl.BlockSpec((tm, d), lambda i: (i, 0)), vec, vec, vec],
        out_specs=pl.BlockSpec((tm, d), lambda i: (i, 0)),
        compiler_params=_params("parallel"),
    )(x, _row(g), _row(scale), _row(shift))


def _rms(y, g, n_real):
    ss = jnp.sum(y * y, axis=-1, keepdims=True) * (1.0 / n_real)
    return y * lax.rsqrt(ss + EPS) * g


def _rope(y, cos, sin_signed, shift):
    up = pltpu.roll(y, LANES - shift, 1)
    dn = pltpu.roll(y, shift, 1)
    lane = lax.broadcasted_iota(jnp.int32, y.shape, 1)
    rot = jnp.where((lane & (2 * shift - 1)) < shift, up, dn)
    return y * cos + rot * sin_signed


def _epi_plain(acc, o_ref):
    o_ref[...] = acc.astype(o_ref.dtype)


def _epi_silu(acc, o_ref):
    o_ref[...] = (acc * jax.nn.sigmoid(acc)).astype(o_ref.dtype)


def _epi_headnorm(acc, g_ref, o_ref):
    for h in range(acc.shape[1] // LANES):
        sl = slice(h * LANES, (h + 1) * LANES)
        o_ref[:, sl] = _rms(acc[:, sl], g_ref[...], LANES).astype(o_ref.dtype)


def _epi_headnorm_rope(acc, g_ref, cos_ref, sin_ref, o_ref):
    for h in range(acc.shape[1] // LANES):
        sl = slice(h * LANES, (h + 1) * LANES)
        y = _rms(acc[:, sl], g_ref[...], LANES)
        o_ref[:, sl] = _rope(y, cos_ref[...], sin_ref[...], HEAD_DIM // 4).astype(o_ref.dtype)


def _epi_rownorm(acc, g_ref, o_ref):
    o_ref[...] = _rms(acc, g_ref[...], acc.shape[1]).astype(o_ref.dtype)


def _epi_mla_pe(acc, g_ref, cos_ref, sin_ref, o_ref):
    y = _rms(acc, g_ref[...], MLA_ROPE)
    o_ref[...] = _rope(y, cos_ref[...], sin_ref[...], MLA_ROPE // 4).astype(o_ref.dtype)


def _epi_mla_q(acc, gn_ref, gp_ref, cos_ref, sin_ref, o_ref):
    for h in range(acc.shape[1] // (2 * LANES)):
        a = slice(2 * h * LANES, (2 * h + 1) * LANES)
        b = slice((2 * h + 1) * LANES, (2 * h + 2) * LANES)
        o_ref[:, a] = _rms(acc[:, a], gn_ref[...], MLA_NOPE).astype(o_ref.dtype)
        y = _rms(acc[:, b], gp_ref[...], MLA_ROPE)
        o_ref[:, b] = _rope(y, cos_ref[...], sin_ref[...], MLA_ROPE // 4).astype(o_ref.dtype)


def _epi_resid(acc, x_ref, gate_ref, o_ref):
    o_ref[...] = x_ref[...] + gate_ref[...] * acc


def _mm_kernel(*refs, epi):
    a_ref, w_ref = refs[0], refs[1]
    acc = jnp.dot(a_ref[...], w_ref[...], preferred_element_type=F32)
    epi(acc, *refs[2:])


def _mm(a, w, epi, extras=(), out_dtype=BF16, tn=512):
    m, k = a.shape
    n = w.shape[1]
    tm = min(m, 1024)
    tn = min(tn, n)
    assert m % tm == 0 and n % tn == 0
    specs = [pl.BlockSpec((tm, k), lambda i, j: (i, 0)), pl.BlockSpec((k, tn), lambda i, j: (0, j))]
    arrs = [a, w]
    for arr, kind in extras:
        if kind == "row":
            specs.append(pl.BlockSpec((1, tn), lambda i, j: (0, j)))
        elif kind == "tile":
            specs.append(pl.BlockSpec((tm, tn), lambda i, j: (i, j)))
        elif kind == "rows":
            specs.append(pl.BlockSpec((tm, arr.shape[1]), lambda i, j: (i, 0)))
        else:
            specs.append(pl.BlockSpec(arr.shape, lambda i, j: (0, 0)))
        arrs.append(arr)
    return pl.pallas_call(
        functools.partial(_mm_kernel, epi=epi),
        out_shape=jax.ShapeDtypeStruct((m, n), out_dtype),
        grid=(m // tm, n // tn),
        in_specs=specs,
        out_specs=pl.BlockSpec((tm, tn), lambda i, j: (i, j)),
        compiler_params=_params("parallel", "parallel"),
    )(*arrs)


def _mm_t_kernel(wt_ref, a_ref, o_ref):
    acc = lax.dot_general(wt_ref[...], a_ref[...], (((1,), (1,)), ((), ())), preferred_element_type=F32)
    for t in range(o_ref.shape[1]):
        o_ref[0, t] = acc[:, t * ATTN_TK:(t + 1) * ATTN_TK].astype(o_ref.dtype)


def _mm_t(a, w, heads):
    m, k = a.shape
    dv = w.shape[1] // heads
    tm = min(m, 1024)
    return pl.pallas_call(
        _mm_t_kernel,
        out_shape=jax.ShapeDtypeStruct((heads, m // ATTN_TK, dv, ATTN_TK), BF16),
        grid=(m // tm, heads),
        in_specs=[pl.BlockSpec((dv, k), lambda i, j: (j, 0)), pl.BlockSpec((tm, k), lambda i, j: (i, 0))],
        out_specs=pl.BlockSpec((1, tm // ATTN_TK, dv, ATTN_TK), lambda i, j: (j, i, 0, 0)),
        compiler_params=_params("parallel", "parallel"),
    )(w.T, a)


def _rope_tables(n_tok, rot_dim, pad_to, identity_rows):
    half = rot_dim // 2
    qr = half // 2
    rows = n_tok // GRID_W
    row = np.repeat(np.arange(rows, dtype=np.float64), GRID_W)
    col = np.tile(np.arange(GRID_W, dtype=np.float64), rows)
    inv = 1.0 / (ROPE_BASE ** (np.arange(0, half, 2, dtype=np.float64) / half))
    ar = row[:, None] * inv[None, :]
    ac = col[:, None] * inv[None, :]
    ang = np.concatenate([ar, ar, ac, ac], axis=-1)
    sign = np.where((np.arange(rot_dim) % (2 * qr)) < qr, -1.0, 1.0)
    cos = np.cos(ang)
    sin = np.sin(ang) * sign[None, :]
    cos = np.concatenate([cos, np.ones((identity_rows, rot_dim))], axis=0)
    sin = np.concatenate([sin, np.zeros((identity_rows, rot_dim))], axis=0)
    pad = ((0, 0), (0, pad_to - rot_dim))
    return (jnp.asarray(np.pad(cos, pad), F32), jnp.asarray(np.pad(sin, pad), F32))


class _AttnCfg:
    def __init__(self, maps, dqk, dv, scale, n_src, kx, sink, window, diff_lam_init, tq, tk):
        self.maps, self.dqk, self.dv, self.scale = maps, dqk, dv, scale
        self.n_src, self.kx, self.sink, self.window = n_src, kx, sink, window
        self.diff_lam_init, self.tq, self.tk = diff_lam_init, tq, tk


def _attn_kernel(*refs, cfg):
    it = iter(refs)
    q_ref = next(it)
    srcs = []
    for _ in range(cfg.n_src):
        k_ref = next(it)
        kx_ref = next(it) if cfg.kx else None
        v_ref = next(it)
        srcs.append((k_ref, kx_ref, v_ref))
    sink_ref = next(it) if cfg.sink else None
    gate_ref = next(it)
    if cfg.diff_lam_init is not None:
        lam_ref, subg_ref = next(it), next(it)
    o_ref = next(it)
    m_ref, l_ref, acc_ref = next(it), next(it), next(it)

    tq, tk, maps, dqk = cfg.tq, cfg.tk, cfg.maps, cfg.dqk
    qi = pl.program_id(1)
    c = cfg.scale * LOG2E

    if cfg.sink:
        m_ref[...] = jnp.full(m_ref.shape, 1.0, F32) * (sink_ref[0, 0:1, 0:1] * LOG2E)
        l_ref[...] = jnp.ones(l_ref.shape, F32)
    else:
        m_ref[...] = jnp.full(m_ref.shape, NEG_INF, F32)
        l_ref[...] = jnp.zeros(l_ref.shape, F32)
    acc_ref[...] = jnp.zeros(acc_ref.shape, F32)

    def make_body(k_ref, kx_ref, v_ref, masked):
        def body(kt, carry):
            koff = pl.multiple_of(kt * tk, tk)
            vt = v_ref[0, kt]
            for mp in range(maps):
                q = q_ref[:, mp * dqk:(mp + 1) * dqk]
                if kx_ref is None:
                    k = k_ref[pl.ds(koff, tk), mp * dqk:(mp + 1) * dqk]
                else:
                    k = jnp.concatenate([k_ref[pl.ds(koff, tk), :], kx_ref[pl.ds(koff, tk), :]], axis=1)
                s = lax.dot_general(k, q, (((1,), (1,)), ((), ())), preferred_element_type=F32) * c
                if masked:
                    kpos = koff + lax.broadcasted_iota(jnp.int32, (tk, tq), 0)
                    qpos = qi * tq + lax.broadcasted_iota(jnp.int32, (tk, tq), 1)
                    dlt = kpos - qpos
                    s = jnp.where((dlt >= -cfg.window) & (dlt <= cfg.window), s, NEG_INF)
                m_prev = m_ref[mp]
                m_new = jnp.maximum(m_prev, jnp.max(s, axis=0, keepdims=True))
                alpha = jnp.exp2(m_prev - m_new)
                p = jnp.exp2(s - m_new)
                l_ref[mp] = alpha * l_ref[mp] + jnp.sum(p, axis=0, keepdims=True)
                acc_ref[mp] = alpha * acc_ref[mp] + jnp.dot(vt, p.astype(BF16), preferred_element_type=F32)
                m_ref[mp] = m_new
            return carry
        return body

    for si, (k_ref, kx_ref, v_ref) in enumerate(srcs):
        nk = k_ref.shape[0] // tk
        if cfg.window is not None and si == 0:
            first = qi * (tq // tk) - 1
            lo = jnp.maximum(first, 0)
            hi = jnp.minimum(first + tq // tk + 2, nk)
            lax.fori_loop(lo, hi, make_body(k_ref, kx_ref, v_ref, True), 0)
        else:
            lax.fori_loop(0, nk, make_body(k_ref, kx_ref, v_ref, False), 0)

    gate = gate_ref[...].astype(F32)
    if cfg.diff_lam_init is None:
        o = (acc_ref[0] / l_ref[0]).T
    else:
        lam = (jnp.exp(jnp.sum(lam_ref[0:1, :] * lam_ref[1:2, :], axis=-1, keepdims=True))
               - jnp.exp(jnp.sum(lam_ref[2:3, :] * lam_ref[3:4, :], axis=-1, keepdims=True))
               + cfg.diff_lam_init)
        o = (acc_ref[0] / l_ref[0] - lam * (acc_ref[1] / l_ref[1])).T
        o = _rms(o, subg_ref[...], cfg.dv) * (1.0 - cfg.diff_lam_init)
    o_ref[...] = (o * gate).astype(o_ref.dtype)


def _attention(q, srcs, gate, *, heads, kv_group, maps, dqk, dv, scale, sink=None, window=None,
               diff=None):
    mq = q.shape[0]
    tq = min(mq, 512)
    tk = ATTN_TK
    kx = srcs[0][1] is not None
    cfg = _AttnCfg(maps, dqk, dv, scale, len(srcs), kx, sink is not None, window,
                   None if diff is None else diff[0], tq, tk)
    kw = maps * dqk - (LANES if kx else 0)
    specs = [pl.BlockSpec((tq, maps * dqk), lambda h, i: (i, h))]
    arrs = [q]
    for k, kxa, v in srcs:
        nk = k.shape[0]
        assert nk % tk == 0
        specs.append(pl.BlockSpec((nk, kw), lambda h, i: (0, h // kv_group)))
        arrs.append(k)
        if kx:
            specs.append(pl.BlockSpec((nk, LANES), lambda h, i: (0, 0)))
            arrs.append(kxa)
        specs.append(pl.BlockSpec((1, nk // tk, dv, tk), lambda h, i: (h // kv_group, 0, 0, 0)))
        arrs.append(v)
    if sink is not None:
        specs.append(pl.BlockSpec((1, 8, LANES), lambda h, i: (h, 0, 0)))
        arrs.append(jnp.broadcast_to(sink.astype(F32)[:, None, None], (heads, 8, LANES)))
    specs.append(pl.BlockSpec((tq, dv), lambda h, i: (i, h)))
    arrs.append(gate)
    if diff is not None:
        specs.append(pl.BlockSpec((4, LANES), lambda h, i: (0, 0)))
        arrs.append(diff[1])
        specs.append(pl.BlockSpec((1, dv), lambda h, i: (0, 0)))
        arrs.append(diff[2])
    return pl.pallas_call(
        functools.partial(_attn_kernel, cfg=cfg),
        out_shape=jax.ShapeDtypeStruct((mq, heads * dv), BF16),
        grid=(heads, mq // tq),
        in_specs=specs,
        out_specs=pl.BlockSpec((tq, dv), lambda h, i: (i, h)),
        scratch_shapes=[pltpu.VMEM((maps, 1, tq), F32), pltpu.VMEM((maps, 1, tq), F32),
                        pltpu.VMEM((maps, dv, tq), F32)],
        compiler_params=_params("parallel", "parallel"),
    )(*arrs)


def _conv3_kernel(u_ref, prev_ref, next_ref, w_ref, b_ref, o_ref):
    i = pl.program_id(0)
    tm = u_ref.shape[0]
    x = u_ref[...]
    row = lax.broadcasted_iota(jnp.int32, x.shape, 0)
    prev_row = jnp.where(i > 0, prev_ref[7:8, :], 0.0)
    next_row = jnp.where(i < pl.num_programs(0) - 1, next_ref[0:1, :], 0.0)
    xm = jnp.where(row == 0, prev_row, pltpu.roll(x, 1, 0))
    xp = jnp.where(row == tm - 1, next_row, pltpu.roll(x, tm - 1, 0))
    o_ref[...] = xm * w_ref[0:1, :] + x * w_ref[1:2, :] + xp * w_ref[2:3, :] + b_ref[...]


def _conv3(u, w, b):
    m, n = u.shape
    tm = min(m, 512)
    tn = 512
    nrb = m // 8
    return pl.pallas_call(
        _conv3_kernel,
        out_shape=jax.ShapeDtypeStruct((m, n), F32),
        grid=(m // tm, n // tn),
        in_specs=[pl.BlockSpec((tm, tn), lambda i, j: (i, j)),
                  pl.BlockSpec((8, tn), lambda i, j: (jnp.maximum(i * (tm // 8) - 1, 0), j)),
                  pl.BlockSpec((8, tn), lambda i, j: (jnp.minimum((i + 1) * (tm // 8), nrb - 1), j)),
                  pl.BlockSpec((3, tn), lambda i, j: (0, j)),
                  pl.BlockSpec((1, tn), lambda i, j: (0, j))],
        out_specs=pl.BlockSpec((tm, tn), lambda i, j: (i, j)),
        compiler_params=_params("parallel", "parallel"),
    )(u, u, u, w.astype(F32), _row(b))


def _split_dot(a, b):
    ah = a.astype(BF16)
    al = (a - ah.astype(F32)).astype(BF16)
    bh = b.astype(BF16)
    bl = (b - bh.astype(F32)).astype(BF16)
    d = functools.partial(jnp.dot, preferred_element_type=F32)
    return d(ah, bh) + (d(ah, bl) + d(al, bh))


def _filter_kernel(z_ref, ts_ref, w1_ref, b1_ref, w2_ref, b2_ref, fr_ref, w3_ref, dl_ref, o_ref, ss_ref):
    h = jnp.sin(fr_ref[0:1, :] * (_split_dot(z_ref[...], w1_ref[...]) + b1_ref[...]))
    h = jnp.sin(fr_ref[1:2, :] * (_split_dot(h, w2_ref[...]) + b2_ref[...]))
    hf = _split_dot(h, w3_ref[0])
    kc = hf * jnp.exp(-ts_ref[:, 0:1] * dl_ref[...]) * ts_ref[:, 1:2]
    o_ref[...] = kc

    @pl.when(pl.program_id(0) == 0)
    def _():
        ss_ref[...] = jnp.zeros(ss_ref.shape, F32)

    ss_ref[...] += jnp.sum(kc * kc, axis=0, keepdims=True)


def _filter_tables(L):
    pos = np.arange(L, dtype=np.float32)
    t = (pos / np.float32(max(L - 1, 1))).astype(np.float32)
    bands = np.linspace(1e-4, HYENA_BANDS - 1, HYENA_BANDS, dtype=np.float32)
    ang = (np.float32(2.0 * math.pi / L) * pos[:, None] * bands[None, :]).astype(np.float32)
    z = np.concatenate([t[:, None], np.cos(ang), -np.sin(ang)], axis=-1).astype(np.float32)
    src = np.concatenate([np.arange(L), np.array([0]), np.arange(L - 1, 0, -1)])
    sign = np.concatenate([np.ones(L), np.zeros(1), -np.ones(L - 1)])
    zt = np.zeros((2 * L, LANES), np.float32)
    zt[:, :z.shape[1]] = z[src]
    ts = np.stack([t[src], sign.astype(np.float32)], axis=1)
    return jnp.asarray(zt), jnp.asarray(ts)


def _hyena_filter(L, f_w1, f_b1, f_w2, f_b2, f_freq, f_w3, width):
    zt, ts = _filter_tables(L)
    hid = f_w1.shape[1]
    w1 = jnp.zeros((LANES, hid), F32).at[:f_w1.shape[0]].set(f_w1.astype(F32))
    w3 = f_w3.astype(F32).reshape(hid, 2, 2, width)
    w3 = jnp.stack([w3[:, :, 0].reshape(hid, 2 * width), w3[:, :, 1].reshape(hid, 2 * width)])
    max_decay = math.log(HYENA_DECAY_TARGET) / HYENA_FAST_DECAY
    min_decay = math.log(HYENA_DECAY_TARGET) / HYENA_SLOW_DECAY
    deltas = np.abs(np.linspace(min_decay, max_decay, width, dtype=np.float32))
    dl = jnp.asarray(np.tile(deltas, 2)[None, :], F32)
    tr = min(L, 512)
    nfw = L // tr
    cw = 2 * width
    full = lambda shp: pl.BlockSpec(shp, lambda i: (0,) * len(shp))
    return pl.pallas_call(
        _filter_kernel,
        out_shape=(jax.ShapeDtypeStruct((2 * L, cw), F32), jax.ShapeDtypeStruct((1, cw), F32)),
        grid=(2 * L // tr,),
        in_specs=[pl.BlockSpec((tr, LANES), lambda i: (i, 0)), pl.BlockSpec((tr, 2), lambda i: (i, 0)),
                  full((LANES, hid)), full((1, hid)), full((hid, hid)), full((1, hid)), full((2, hid)),
                  pl.BlockSpec((1, hid, cw), lambda i: (i // nfw, 0, 0)), full((1, cw))],
        out_specs=(pl.BlockSpec((tr, cw), lambda i: (i, 0)), full((1, cw))),
        compiler_params=_params("arbitrary"),
    )(zt, ts, w1, _row(f_b1), f_w2.astype(F32), _row(f_b2), f_freq.astype(F32), w3, dl)


@functools.lru_cache(maxsize=None)
def _dft_mats():
    r = DFT_R
    n = r * r
    k1 = np.arange(r)[:, None]
    n1 = np.arange(r)[None, :]
    base = np.exp(-2j * np.pi * n1 * (k1 + 0.5) / r)
    n2 = np.arange(r)[:, None, None]
    tw = np.exp(-2j * np.pi * n2 * (k1[None] + 0.5) / n)
    ma = tw * base[None]
    ma = np.concatenate([ma.real, ma.imag], axis=1)
    k2 = np.arange(r // 2)[:, None]
    fb = np.exp(-2j * np.pi * np.arange(r)[None, :] * k2 / r)
    fb = np.block([[fb.real, -fb.imag], [fb.imag, fb.real]])
    mc = np.conj(fb[: r // 2, :r] + 1j * fb[r // 2:, :r]).T
    mc = np.block([[mc.real, -mc.imag], [mc.imag, mc.real]])
    md = np.conj(tw * base[None]).transpose(0, 2, 1)[:, : r // 2, :] * (2.0 / n)
    md = np.concatenate([md.real, -md.imag], axis=2)
    return tuple(np.asarray(m, np.float32) for m in (ma, fb, mc, md))


def _dft_a_kernel(m_ref, x_ref, o_ref, *, nb, ch):
    for j in range(nb):
        xs = x_ref[:, j * ch:(j + 1) * ch].astype(BF16)
        o_ref[:, j * ch:(j + 1) * ch] = jnp.dot(m_ref[j], xs, preferred_element_type=F32).astype(BF16)


def _dft_stage_a(x2, ch):
    r = DFT_R
    kdim = x2.shape[0]
    ma = _const_bf(_dft_mats()[0][:, :, :kdim])
    nb = max(1, min(r, 16384 // ch))
    return pl.pallas_call(
        functools.partial(_dft_a_kernel, nb=nb, ch=ch),
        out_shape=jax.ShapeDtypeStruct((2 * r, r * ch), BF16),
        grid=(r // nb,),
        in_specs=[pl.BlockSpec((nb, 2 * r, kdim), lambda i: (i, 0, 0)),
                  pl.BlockSpec((kdim, nb * ch), lambda i: (0, i))],
        out_specs=pl.BlockSpec((2 * r, nb * ch), lambda i: (0, i)),
        compiler_params=_params("parallel"),
    )(ma, x2)


def _dft_b_kernel(fb_ref, a_ref, ss_ref, o_ref, *, kb):
    rn = lax.rsqrt(ss_ref[...] + EPS)
    for j in range(kb):
        op = jnp.concatenate([a_ref[0, j], a_ref[1, j]], axis=0)
        o_ref[j] = jnp.dot(fb_ref[...], op, preferred_element_type=F32) * rn


def _dft_stage_b_spectrum(a2, ss, ch):
    r = DFT_R
    fb = _const_bf(_dft_mats()[1])
    kb = max(1, min(r, 8192 // ch))
    a4 = a2.reshape(2, r, r, ch)
    return pl.pallas_call(
        functools.partial(_dft_b_kernel, kb=kb),
        out_shape=jax.ShapeDtypeStruct((r, r, ch), F32),
        grid=(r // kb,),
        in_specs=[pl.BlockSpec((r, 2 * r), lambda i: (0, 0)),
                  pl.BlockSpec((2, kb, r, ch), lambda i: (0, i, 0, 0)),
                  pl.BlockSpec((1, ch), lambda i: (0, 0))],
        out_specs=pl.BlockSpec((kb, r, ch), lambda i: (i, 0, 0)),
        compiler_params=_params("parallel"),
    )(fb, a4, ss)


def _cmul(x, s):
    h = x.shape[0] // 2
    xr, xi, sr, si = x[:h], x[h:], s[:h], s[h:]
    return jnp.concatenate([xr * sr - xi * si, xr * si + xi * sr], axis=0)


def _dft_bc_kernel(fb_ref, mc_ref, a_ref, s_ref, o_ref, *, kb, ch):
    for j in range(kb):
        op = jnp.concatenate([a_ref[0, j], a_ref[1, j]], axis=0)
        xs = jnp.dot(fb_ref[...], op, preferred_element_type=F32)
        y = _cmul(xs, s_ref[j]).astype(BF16)
        o_ref[:, j * ch:(j + 1) * ch] = jnp.dot(mc_ref[...], y, preferred_element_type=F32).astype(BF16)


def _dft_stage_bc(a2, spec, order, ch):
    r = DFT_R
    fb, mc = (_const_bf(m) for m in _dft_mats()[1:3])
    kb = max(1, min(r, 8192 // ch))
    a4 = a2.reshape(2, r, r, ch)
    return pl.pallas_call(
        functools.partial(_dft_bc_kernel, kb=kb, ch=ch),
        out_shape=jax.ShapeDtypeStruct((2 * r, r * ch), BF16),
        grid=(r // kb,),
        in_specs=[pl.BlockSpec((r, 2 * r), lambda i: (0, 0)),
                  pl.BlockSpec((2 * r, r), lambda i: (0, 0)),
                  pl.BlockSpec((2, kb, r, ch), lambda i: (0, i, 0, 0)),
                  pl.BlockSpec((kb, r, ch), lambda i: (i, 0, order))],
        out_specs=pl.BlockSpec((2 * r, kb * ch), lambda i: (0, i)),
        compiler_params=_params("parallel"),
    )(fb, mc, a4, spec)


def _dft_d_kernel(*refs, nb, ch, gated):
    md_ref, c_ref, u_ref, mul_ref, skip_ref = refs[:5]
    gate_ref = refs[5] if gated else None
    o_ref = refs[-1]
    for j in range(nb):
        sl = slice(j * ch, (j + 1) * ch)
        op = jnp.concatenate([c_ref[0, j], c_ref[1, j]], axis=0)
        y = jnp.dot(md_ref[j], op, preferred_element_type=F32)
        z = mul_ref[:, sl] * (y + u_ref[:, sl] * skip_ref[...])
        if gated:
            z = z * gate_ref[:, sl].astype(F32)
        o_ref[:, sl] = z.astype(o_ref.dtype)


def _dft_stage_d(c2, u2, mul2, skip, gate2, ch):
    r = DFT_R
    md = _const_bf(_dft_mats()[3])
    nb = max(1, min(r, 8192 // ch))
    c4 = c2.reshape(2, r, r, ch)
    blk = pl.BlockSpec((r // 2, nb * ch), lambda i: (0, i))
    specs = [pl.BlockSpec((nb, r // 2, 2 * r), lambda i: (i, 0, 0)),
             pl.BlockSpec((2, nb, r, ch), lambda i: (0, i, 0, 0)),
             blk, blk, pl.BlockSpec((1, ch), lambda i: (0, 0))]
    arrs = [md, c4, u2, mul2, skip]
    if gate2 is not None:
        specs.append(blk)
        arrs.append(gate2)
    return pl.pallas_call(
        functools.partial(_dft_d_kernel, nb=nb, ch=ch, gated=gate2 is not None),
        out_shape=jax.ShapeDtypeStruct((r // 2, r * ch), F32 if gate2 is None else BF16),
        grid=(r // nb,),
        in_specs=specs,
        out_specs=blk,
        compiler_params=_params("parallel"),
    )(*arrs)


def _hyena_long(u, gate, filt, ss, skip, width):
    L = u.shape[0]
    r = DFT_R
    assert 2 * L == r * r
    cw = 2 * width
    spec = _dft_stage_b_spectrum(_dft_stage_a(filt.reshape(r, r * cw), cw), ss, cw)
    lay = lambda t: t.reshape(r // 2, r * width)
    v, x1, x2 = (lay(u[:, k * width:(k + 1) * width]) for k in range(3))
    sk = skip.astype(F32)
    c2 = _dft_stage_bc(_dft_stage_a(v, width), spec, 0, width)
    z = _dft_stage_d(c2, v, x1, sk[0:1], None, width)
    c2 = _dft_stage_bc(_dft_stage_a(z, width), spec, 1, width)
    z = _dft_stage_d(c2, z, x2, sk[1:2], lay(gate), width)
    return z.reshape(L, width)


@functools.lru_cache(maxsize=None)
def _short_dft_mats(L):
    n = 2 * L
    k = np.arange(L)[:, None]
    t = np.arange(n)[None, :]
    f = np.exp(-2j * np.pi * t * (k + 0.5) / n)
    fwd = np.concatenate([f.real, f.imag], axis=0)
    inv = np.conj(f[:, :L]).T * (2.0 / n)
    inv = np.concatenate([inv.real, -inv.imag], axis=1)
    return np.asarray(fwd, np.float32), np.asarray(inv, np.float32)


def _short_spec_kernel(f_ref, k_ref, ss_ref, o_ref):
    o_ref[...] = jnp.dot(f_ref[...], k_ref[...].astype(BF16), preferred_element_type=F32) * lax.rsqrt(ss_ref[...] + EPS)


def _short_conv_kernel(*refs, gated):
    f_ref, i_ref, u_ref, s_ref, mul_ref, skip_ref = refs[:6]
    gate_ref = refs[6] if gated else None
    o_ref = refs[-1]
    u = u_ref[...]
    xs = jnp.dot(f_ref[...], u.astype(BF16), preferred_element_type=F32)
    y = jnp.dot(i_ref[...], _cmul(xs, s_ref[...]).astype(BF16), preferred_element_type=F32)
    z = mul_ref[...] * (y + u * skip_ref[...])
    if gated:
        z = z * gate_ref[...].astype(F32)
    o_ref[...] = z.astype(o_ref.dtype)


def _hyena_short(u, gate, filt, ss, skip, width):
    L = u.shape[0]
    fwd, inv = (_const_bf(m) for m in _short_dft_mats(L))
    cw = 2 * width
    tn = 512
    spec = pl.pallas_call(
        _short_spec_kernel,
        out_shape=jax.ShapeDtypeStruct((2 * L, cw), F32),
        grid=(cw // tn,),
        in_specs=[pl.BlockSpec((2 * L, 2 * L), lambda j: (0, 0)), pl.BlockSpec((2 * L, tn), lambda j: (0, j)),
                  pl.BlockSpec((1, tn), lambda j: (0, j))],
        out_specs=pl.BlockSpec((2 * L, tn), lambda j: (0, j)),
        compiler_params=_params("parallel"),
    )(fwd, filt, ss)

    def conv(uarr, ucol, order, mulcol, gate_arr):
        nt = width // tn
        specs = [pl.BlockSpec((2 * L, L), lambda j: (0, 0)), pl.BlockSpec((L, 2 * L), lambda j: (0, 0)),
                 pl.BlockSpec((L, tn), lambda j: (0, ucol * nt + j)),
                 pl.BlockSpec((2 * L, tn), lambda j: (0, order * nt + j)),
                 pl.BlockSpec((L, tn), lambda j: (0, mulcol * nt + j)),
                 pl.BlockSpec((1, tn), lambda j: (0, order * nt + j))]
        arrs = [fwd[:, :L], inv, uarr, spec, u, skip.astype(F32).reshape(1, cw)]
        if gate_arr is not None:
            specs.append(pl.BlockSpec((L, tn), lambda j: (0, j)))
            arrs.append(gate_arr)
        return pl.pallas_call(
            functools.partial(_short_conv_kernel, gated=gate_arr is not None),
            out_shape=jax.ShapeDtypeStruct((L, width), F32 if gate_arr is None else BF16),
            grid=(nt,),
            in_specs=specs,
            out_specs=pl.BlockSpec((L, tn), lambda j: (0, j)),
            compiler_params=_params("parallel"),
        )(*arrs)

    z = conv(u, 0, 0, 1, None)
    return conv(z, 0, 1, 2, gate)


def _swa_layer(h, hc, w_in, q_g, k_g, sink, need_ctx):
    n, c = h.shape[0], hc.shape[0]
    hq, hk, d = 16, 4, HEAD_DIM
    wq, wk, wv, wg = (_bf(w_in[:, a:b]) for a, b in
                      ((0, hq * d), (hq * d, (hq + hk) * d), ((hq + hk) * d, (hq + 2 * hk) * d),
                       ((hq + 2 * hk) * d, 2 * hq * d + 2 * hk * d)))
    cos, sin = _rope_tables(n, d, d, 0)
    qg, kg = _row(q_g), _row(k_g)
    rope = lambda g: [(g, "const"), (cos, "rows"), (sin, "rows")]
    q = _mm(h, wq, _epi_headnorm_rope, rope(qg))
    k = _mm(h, wk, _epi_headnorm_rope, rope(kg))
    v = _mm_t(h, wv, hk)
    g = _mm(h, wg, _epi_silu)
    ck = _mm(hc, wk, _epi_headnorm, [(kg, "const")])
    cv = _mm_t(hc, wv, hk)
    kw = dict(heads=hq, kv_group=hq // hk, maps=1, dqk=d, dv=d, scale=d ** -0.5, sink=sink)
    o = _attention(q, [(k, None, v), (ck, None, cv)], g, window=SWA_WINDOW, **kw)
    oc = None
    if need_ctx:
        cq = _mm(hc, wq, _epi_headnorm, [(qg, "const")])
        cg = _mm(hc, wg, _epi_silu)
        oc = _attention(cq, [(ck, None, cv)], cg, **kw)
    return o, oc


def _mla_layer(h, hc, w_in, qa_g, kva_g, w_qb, w_kvb, qn_nope_g, qn_pe_g, kn_nope_g, kn_pe_g, need_ctx):
    n, c = h.shape[0], hc.shape[0]
    nh = 16
    qr, kvr, rp, nope = MLA_Q_RANK, MLA_KV_RANK, MLA_ROPE, MLA_NOPE
    w_cq = _bf(w_in[:, :qr])
    w_ckv = _bf(w_in[:, qr:qr + kvr])
    w_kr = _bf(jnp.pad(w_in[:, qr + kvr:qr + kvr + rp], ((0, 0), (0, LANES - rp))))
    w_g = _bf(w_in[:, qr + kvr + rp:])
    wqb = jnp.pad(w_qb.reshape(qr, nh, nope + rp), ((0, 0), (0, 0), (0, 2 * LANES - nope - rp)))
    wqb = _bf(wqb.reshape(qr, nh * 2 * LANES))
    wkv = w_kvb.reshape(kvr, nh, 2 * LANES)
    w_kn, w_v = _bf(wkv[:, :, :nope].reshape(kvr, nh * nope)), _bf(wkv[:, :, nope:].reshape(kvr, nh * LANES))
    pad64 = lambda g: jnp.pad(_row(g), ((0, 0), (0, LANES - rp)))
    gqp, gkp = pad64(qn_pe_g), pad64(kn_pe_g)
    cos, sin = _rope_tables(n, rp, LANES, 0)
    cos_c = jnp.pad(jnp.ones((c, rp), F32), ((0, 0), (0, LANES - rp)))
    sin_c = jnp.zeros((c, LANES), F32)

    def side(hh, cs, sn, queries):
        ckv = _mm(hh, w_ckv, _epi_rownorm, [(_row(kva_g), "row")], tn=kvr)
        kp = _mm(hh, w_kr, _epi_mla_pe, [(gkp, "const"), (cs, "rows"), (sn, "rows")])
        kn = _mm(ckv, w_kn, _epi_headnorm, [(_row(kn_nope_g), "const")])
        v = _mm_t(ckv, w_v, nh)
        if not queries:
            return kn, kp, v, None, None
        cq = _mm(hh, w_cq, _epi_rownorm, [(_row(qa_g), "row")], tn=qr)
        q = _mm(cq, wqb, _epi_mla_q, [(_row(qn_nope_g), "const"), (gqp, "const"), (cs, "rows"), (sn, "rows")])
        g = _mm(hh, w_g, _epi_silu)
        return kn, kp, v, q, g

    kn, kp, v, q, g = side(h, cos, sin, True)
    ckn, ckp, cv, cq, cg = side(hc, cos_c, sin_c, need_ctx)
    kw = dict(heads=nh, kv_group=1, maps=1, dqk=2 * LANES, dv=LANES, scale=(nope + rp) ** -0.5)
    o = _attention(q, [(kn, kp, v), (ckn, ckp, cv)], g, **kw)
    oc = _attention(cq, [(ckn, ckp, cv)], cg, **kw) if need_ctx else None
    return o, oc


def _hyena_layer(h, hc, w_in, conv_w, conv_b, f_w1, f_b1, f_w2, f_b2, f_freq, f_w3, skip, need_ctx):
    width = skip.shape[1]
    w_u, w_g = _bf(w_in[:, :3 * width]), _bf(w_in[:, 3 * width:])

    def branch(hh, long):
        L = hh.shape[0]
        u = _conv3(_mm(hh, w_u, _epi_plain, out_dtype=F32), conv_w, conv_b)
        g = _mm(hh, w_g, _epi_silu)
        filt, ss = _hyena_filter(L, f_w1, f_b1, f_w2, f_b2, f_freq, f_w3, width)
        return (_hyena_long if long else _hyena_short)(u, g, filt, ss, skip, width)

    o = branch(h, True)
    oc = branch(hc, False) if need_ctx else None
    return o, oc


def _diff_layer(h, hc, w_in, q_g, k_g, lq1, lk1, lq2, lk2, subln_g, lam_init, need_ctx):
    n, c = h.shape[0], hc.shape[0]
    nh, d = 8, HEAD_DIM
    w = nh * 2 * d
    wq, wk, wv, wg = (_bf(w_in[:, i * w:(i + 1) * w]) for i in range(4))
    cos, sin = _rope_tables(n, d, d, 0)
    qg, kg = _row(q_g), _row(k_g)
    rope = lambda g: [(g, "const"), (cos, "rows"), (sin, "rows")]
    q = _mm(h, wq, _epi_headnorm_rope, rope(qg))
    k = _mm(h, wk, _epi_headnorm_rope, rope(kg))
    v = _mm_t(h, wv, nh)
    g = _mm(h, wg, _epi_silu)
    ck = _mm(hc, wk, _epi_headnorm, [(kg, "const")])
    cv = _mm_t(hc, wv, nh)
    lam_vecs = jnp.stack([lq1, lk1, lq2, lk2]).astype(F32)
    kw = dict(heads=nh, kv_group=1, maps=2, dqk=d, dv=2 * d, scale=d ** -0.5,
              diff=(lam_init, lam_vecs, _row(subln_g)))
    o = _attention(q, [(k, None, v), (ck, None, cv)], g, **kw)
    oc = None
    if need_ctx:
        cq = _mm(hc, wq, _epi_headnorm, [(qg, "const")])
        cg = _mm(hc, wg, _epi_silu)
        oc = _attention(cq, [(ck, None, cv)], cg, **kw)
    return o, oc


def kernel(x, c, ctx, c_ctx, norm_g, ada_w, ada_b, swa_w_in, swa_q_g, swa_k_g, swa_sink, swa_w_out, mla_w_in, mla_qa_g, mla_kva_g, mla_w_qb, mla_w_kvb, mla_qn_nope_g, mla_qn_pe_g, mla_kn_nope_g, mla_kn_pe_g, mla_w_out, hyena_w_in, hyena_conv_w, hyena_conv_b, hyena_f_w1, hyena_f_b1, hyena_f_w2, hyena_f_b2, hyena_f_freq, hyena_f_w3, hyena_skip, hyena_w_out, diff_w_in, diff_q_g, diff_k_g, diff_lq1, diff_lk1, diff_lq2, diff_lk2, diff_subln_g, diff_w_out):
    depth, d = norm_g.shape
    assert x.shape[0] == 1
    xs, cs = x[0], ctx[0]
    mods = _adaln(c, c_ctx, ada_w, ada_b)
    for i in range(depth):
        kind, j = i % 4, i // 4
        need_ctx = i < depth - 1
        shift, scale, gate = (mods[i, 0, a * d:(a + 1) * d] for a in range(3))
        shift_c, scale_c, gate_c = (mods[i, 1, a * d:(a + 1) * d] for a in range(3))
        h = _modnorm(xs, norm_g[i], scale, shift)
        hc = _modnorm(cs, norm_g[i], scale_c, shift_c)
        if kind == 0:
            o, oc = _swa_layer(h, hc, swa_w_in[j], swa_q_g[j], swa_k_g[j], swa_sink[j], need_ctx)
            w_out = swa_w_out[j]
        elif kind == 1:
            o, oc = _mla_layer(h, hc, mla_w_in[j], mla_qa_g[j], mla_kva_g[j], mla_w_qb[j], mla_w_kvb[j],
                               mla_qn_nope_g[j], mla_qn_pe_g[j], mla_kn_nope_g[j], mla_kn_pe_g[j], need_ctx)
            w_out = mla_w_out[j]
        elif kind == 2:
            o, oc = _hyena_layer(h, hc, hyena_w_in[j], hyena_conv_w[j], hyena_conv_b[j], hyena_f_w1[j],
                                 hyena_f_b1[j], hyena_f_w2[j], hyena_f_b2[j], hyena_f_freq[j], hyena_f_w3[j],
                                 hyena_skip[j], need_ctx)
            w_out = hyena_w_out[j]
        else:
            lam_init = 0.8 - 0.6 * math.exp(-0.3 * i)
            o, oc = _diff_layer(h, hc, diff_w_in[j], diff_q_g[j], diff_k_g[j], diff_lq1[j], diff_lk1[j],
                                diff_lq2[j], diff_lk2[j], diff_subln_g[j], lam_init, need_ctx)
            w_out = diff_w_out[j]
        wo = _bf(w_out)
        xs = _mm(o, wo, _epi_resid, [(xs, "tile"), (_row(gate), "row")], out_dtype=F32)
        if need_ctx:
            cs = _mm(oc, wo, _epi_resid, [(cs, "tile"), (_row(gate_c), "row")], out_dtype=F32)
    return xs[None]
```

```python
import functools
import math

import numpy as np
import jax
import jax.numpy as jnp
from jax import lax
from jax.experimental import pallas as pl
from jax.experimental.pallas import tpu as pltpu

F32 = jnp.float32
BF16 = jnp.bfloat16

EPS = 1e-6
NEG_INF = -1e30
LOG2E = 1.4426950408889634
LANES = 128
VMEM_LIMIT_BYTES = 56 * 1024 * 1024

GRID_W = 64
ROPE_BASE = 10000.0
HEAD_DIM = 128
SWA_WINDOW = 128
MLA_Q_RANK, MLA_KV_RANK, MLA_NOPE, MLA_ROPE = 512, 256, 128, 64
HYENA_BANDS = 16
HYENA_DECAY_TARGET, HYENA_FAST_DECAY, HYENA_SLOW_DECAY = 1e-2, 0.3, 1.5
DFT_R = 128
ATTN_TK = 256
ATTN_TQ = 512


def _params(*sem):
    return pltpu.CompilerParams(dimension_semantics=sem, vmem_limit_bytes=VMEM_LIMIT_BYTES)


def _row(v):
    return v.reshape(1, -1).astype(F32)


def _bf(w):
    return w.astype(BF16)


def _const_bf(m):
    return jnp.asarray(m, F32).astype(BF16)


def _adaln_kernel(c_ref, w_ref, b_ref, o_ref):
    cv = c_ref[...]
    cv = cv * jax.nn.sigmoid(cv)
    w = w_ref[0]
    r0 = jnp.sum(w * cv[:, 0:1], axis=0, keepdims=True)
    r1 = jnp.sum(w * cv[:, 1:2], axis=0, keepdims=True)
    o_ref[0] = jnp.concatenate([r0, r1], axis=0) + b_ref[0]


def _adaln(c, c_ctx, ada_w, ada_b):
    depth, d, n3 = ada_w.shape
    tn = 512
    cc = jnp.stack([c.reshape(d), c_ctx.reshape(d)], axis=1).astype(F32)
    return pl.pallas_call(
        _adaln_kernel,
        out_shape=jax.ShapeDtypeStruct((depth, 2, n3), F32),
        grid=(depth, n3 // tn),
        in_specs=[pl.BlockSpec((d, 2), lambda l, j: (0, 0)),
                  pl.BlockSpec((1, d, tn), lambda l, j: (l, 0, j)),
                  pl.BlockSpec((1, 1, tn), lambda l, j: (l, 0, j))],
        out_specs=pl.BlockSpec((1, 2, tn), lambda l, j: (l, 0, j)),
        compiler_params=_params("parallel", "parallel"),
    )(cc, ada_w, ada_b.reshape(depth, 1, n3))


def _modnorm_kernel(x_ref, g_ref, sc_ref, sh_ref, o_ref):
    x = x_ref[...]
    y = x * lax.rsqrt(jnp.mean(x * x, axis=-1, keepdims=True) + EPS) * g_ref[...]
    o_ref[...] = (y * (1.0 + sc_ref[...]) + sh_ref[...]).astype(o_ref.dtype)


def _modnorm(x, g, scale, shift):
    m, d = x.shape
    tm = min(m, 512)
    vec = pl.BlockSpec((1, d), lambda i: (0, 0))
    return pl.pallas_call(
        _modnorm_kernel,
        out_shape=jax.ShapeDtypeStruct((m, d), BF16),
        grid=(m // tm,),
        in_specs=[pl.BlockSpec((tm, d), lambda i: (i, 0)), vec, vec, vec],
        out_specs=pl.BlockSpec((tm, d), lambda i: (i, 0)),
        compiler_params=_params("parallel"),
    )(x, _row(g), _row(scale), _row(shift))


def _rms(y, g, n_real):
    ss = jnp.sum(y * y, axis=-1, keepdims=True) * (1.0 / n_real)
    return y * lax.rsqrt(ss + EPS) * g


def _rope(y, cos, sin_signed, shift):
    up = pltpu.roll(y, LANES - shift, 1)
    dn = pltpu.roll(y, shift, 1)
    lane = lax.broadcasted_iota(jnp.int32, y.shape, 1)
    rot = jnp.where((lane & (2 * shift - 1)) < shift, up, dn)
    return y * cos + rot * sin_signed


def _epi_plain(acc, o_ref):
    o_ref[...] = acc.astype(o_ref.dtype)


def _epi_silu(acc, o_ref):
    o_ref[...] = (acc * jax.nn.sigmoid(acc)).astype(o_ref.dtype)


def _epi_headnorm(acc, g_ref, o_ref):
    for h in range(acc.shape[1] // LANES):
        sl = slice(h * LANES, (h + 1) * LANES)
        o_ref[:, sl] = _rms(acc[:, sl], g_ref[...], LANES).astype(o_ref.dtype)


def _epi_headnorm_rope(acc, g_ref, cos_ref, sin_ref, o_ref):
    for h in range(acc.shape[1] // LANES):
        sl = slice(h * LANES, (h + 1) * LANES)
        y = _rms(acc[:, sl], g_ref[...], LANES)
        o_ref[:, sl] = _rope(y, cos_ref[...], sin_ref[...], HEAD_DIM // 4).astype(o_ref.dtype)


def _epi_rownorm(acc, g_ref, o_ref):
    o_ref[...] = _rms(acc, g_ref[...], acc.shape[1]).astype(o_ref.dtype)


def _epi_mla_pe(acc, g_ref, cos_ref, sin_ref, o_ref):
    y = _rms(acc, g_ref[...], MLA_ROPE)
    o_ref[...] = _rope(y, cos_ref[...], sin_ref[...], MLA_ROPE // 4).astype(o_ref.dtype)


def _epi_mla_q(acc, gn_ref, gp_ref, cos_ref, sin_ref, o_ref):
    for h in range(acc.shape[1] // (2 * LANES)):
        a = slice(2 * h * LANES, (2 * h + 1) * LANES)
        b = slice((2 * h + 1) * LANES, (2 * h + 2) * LANES)
        o_ref[:, a] = _rms(acc[:, a], gn_ref[...], MLA_NOPE).astype(o_ref.dtype)
        y = _rms(acc[:, b], gp_ref[...], MLA_ROPE)
        o_ref[:, b] = _rope(y, cos_ref[...], sin_ref[...], MLA_ROPE // 4).astype(o_ref.dtype)


def _epi_resid(acc, x_ref, gate_ref, o_ref):
    o_ref[...] = x_ref[...] + gate_ref[...] * acc


def _mm_kernel(*refs, epi):
    a_ref, w_ref = refs[0], refs[1]
    acc = jnp.dot(a_ref[...], w_ref[...], preferred_element_type=F32)
    epi(acc, *refs[2:])


def _mm(a, w, epi, extras=(), out_dtype=BF16, tn=512):
    m, k = a.shape
    n = w.shape[1]
    tm = min(m, 1024)
    tn = min(tn, n)
    assert m % tm == 0 and n % tn == 0
    specs = [pl.BlockSpec((tm, k), lambda i, j: (i, 0)), pl.BlockSpec((k, tn), lambda i, j: (0, j))]
    arrs = [a, w]
    for arr, kind in extras:
        if kind == "row":
            specs.append(pl.BlockSpec((1, tn), lambda i, j: (0, j)))
        elif kind == "tile":
            specs.append(pl.BlockSpec((tm, tn), lambda i, j: (i, j)))
        elif kind == "rows":
            specs.append(pl.BlockSpec((tm, arr.shape[1]), lambda i, j: (i, 0)))
        else:
            specs.append(pl.BlockSpec(arr.shape, lambda i, j: (0, 0)))
        arrs.append(arr)
    return pl.pallas_call(
        functools.partial(_mm_kernel, epi=epi),
        out_shape=jax.ShapeDtypeStruct((m, n), out_dtype),
        grid=(m // tm, n // tn),
        in_specs=specs,
        out_specs=pl.BlockSpec((tm, tn), lambda i, j: (i, j)),
        compiler_params=_params("parallel", "parallel"),
    )(*arrs)


def _mm_t_kernel(wt_ref, a_ref, o_ref):
    acc = lax.dot_general(wt_ref[...], a_ref[...], (((1,), (1,)), ((), ())), preferred_element_type=F32)
    for t in range(o_ref.shape[1]):
        o_ref[0, t] = acc[:, t * ATTN_TK:(t + 1) * ATTN_TK].astype(o_ref.dtype)


def _mm_t(a, w, heads):
    m, k = a.shape
    dv = w.shape[1] // heads
    tm = min(m, 1024)
    return pl.pallas_call(
        _mm_t_kernel,
        out_shape=jax.ShapeDtypeStruct((heads, m // ATTN_TK, dv, ATTN_TK), BF16),
        grid=(m // tm, heads),
        in_specs=[pl.BlockSpec((dv, k), lambda i, j: (j, 0)), pl.BlockSpec((tm, k), lambda i, j: (i, 0))],
        out_specs=pl.BlockSpec((1, tm // ATTN_TK, dv, ATTN_TK), lambda i, j: (j, i, 0, 0)),
        compiler_params=_params("parallel", "parallel"),
    )(w.T, a)


def _rope_tables(n_tok, rot_dim, pad_to, identity_rows):
    half = rot_dim // 2
    qr = half // 2
    rows = n_tok // GRID_W
    row = np.repeat(np.arange(rows, dtype=np.float64), GRID_W)
    col = np.tile(np.arange(GRID_W, dtype=np.float64), rows)
    inv = 1.0 / (ROPE_BASE ** (np.arange(0, half, 2, dtype=np.float64) / half))
    ar = row[:, None] * inv[None, :]
    ac = col[:, None] * inv[None, :]
    ang = np.concatenate([ar, ar, ac, ac], axis=-1)
    sign = np.where((np.arange(rot_dim) % (2 * qr)) < qr, -1.0, 1.0)
    cos = np.cos(ang)
    sin = np.sin(ang) * sign[None, :]
    cos = np.concatenate([cos, np.ones((identity_rows, rot_dim))], axis=0)
    sin = np.concatenate([sin, np.zeros((identity_rows, rot_dim))], axis=0)
    pad = ((0, 0), (0, pad_to - rot_dim))
    return (jnp.asarray(np.pad(cos, pad), F32), jnp.asarray(np.pad(sin, pad), F32))


class _AttnCfg:
    def __init__(self, maps, dqk, dv, scale, n_src, kx, sink, window, diff_lam_init, tq, tk):
        self.maps, self.dqk, self.dv, self.scale = maps, dqk, dv, scale
        self.n_src, self.kx, self.sink, self.window = n_src, kx, sink, window
        self.diff_lam_init, self.tq, self.tk = diff_lam_init, tq, tk


def _attn_kernel(*refs, cfg):
    it = iter(refs)
    q_ref = next(it)
    srcs = []
    for _ in range(cfg.n_src):
        k_ref = next(it)
        kx_ref = next(it) if cfg.kx else None
        v_ref = next(it)
        srcs.append((k_ref, kx_ref, v_ref))
    sink_ref = next(it) if cfg.sink else None
    gate_ref = next(it)
    if cfg.diff_lam_init is not None:
        lam_ref, subg_ref = next(it), next(it)
    o_ref = next(it)
    m_ref, l_ref, acc_ref = next(it), next(it), next(it)

    tq, tk, maps, dqk = cfg.tq, cfg.tk, cfg.maps, cfg.dqk
    qi = pl.program_id(1)
    c = cfg.scale * LOG2E

    if cfg.sink:
        m_ref[...] = jnp.full(m_ref.shape, 1.0, F32) * (sink_ref[0, 0:1, 0:1] * LOG2E)
        l_ref[...] = jnp.ones(l_ref.shape, F32)
    else:
        m_ref[...] = jnp.full(m_ref.shape, NEG_INF, F32)
        l_ref[...] = jnp.zeros(l_ref.shape, F32)
    acc_ref[...] = jnp.zeros(acc_ref.shape, F32)

    def tiles(k_ref, kx_ref, v_ref, kts, band):
        m = [m_ref[mp] for mp in range(maps)]
        l = [l_ref[mp] for mp in range(maps)]
        acc = [acc_ref[mp] for mp in range(maps)]
        def scores(u):
            koff = pl.multiple_of(kts[u] * tk, tk)
            out = []
            for mp in range(maps):
                q = q_ref[:, mp * dqk:(mp + 1) * dqk]
                if kx_ref is None:
                    k = k_ref[pl.ds(koff, tk), mp * dqk:(mp + 1) * dqk]
                else:
                    k = jnp.concatenate([k_ref[pl.ds(koff, tk), :], kx_ref[pl.ds(koff, tk), :]], axis=1)
                out.append(lax.dot_general(k, q, (((1,), (1,)), ((), ())), preferred_element_type=F32))
            return out

        s_next = scores(0)
        for u, kt in enumerate(kts):
            s_cur = s_next
            if u + 1 < len(kts):
                s_next = scores(u + 1)
            vt = v_ref[0, kt]
            for mp in range(maps):
                s = s_cur[mp] * c
                if band is not None:
                    kpos = band[u][0] * tk + lax.broadcasted_iota(jnp.int32, (tk, tq), 0)
                    qpos = qi * tq + lax.broadcasted_iota(jnp.int32, (tk, tq), 1)
                    dlt = kpos - qpos
                    s = jnp.where((dlt >= -band[u][1]) & (dlt <= band[u][1]), s, NEG_INF)
                m_new = jnp.maximum(m[mp], jnp.max(s, axis=0, keepdims=True))
                alpha = jnp.exp2(m[mp] - m_new)
                p = jnp.exp2(s - m_new)
                l[mp] = alpha * l[mp] + jnp.sum(p, axis=0, keepdims=True)
                acc[mp] = alpha * acc[mp] + jnp.dot(vt, p.astype(BF16), preferred_element_type=F32)
                m[mp] = m_new
        for mp in range(maps):
            m_ref[mp], l_ref[mp], acc_ref[mp] = m[mp], l[mp], acc[mp]

    for si, (k_ref, kx_ref, v_ref) in enumerate(srcs):
        nk = k_ref.shape[0] // tk
        if cfg.window is not None and si == 0:
            kts, band = [], []
            for u in range(tq // tk + 2):
                raw = qi * (tq // tk) - 1 + u
                inside = (raw >= 0) & (raw < nk)
                kts.append(jnp.clip(raw, 0, nk - 1))
                band.append((raw, jnp.where(inside, cfg.window, -1)))
            tiles(k_ref, kx_ref, v_ref, kts, band)
        else:
            unroll = max(u for u in (8 // maps, 4 // maps, 2, 1) if nk % u == 0)

            def body(j, carry, refs=(k_ref, kx_ref, v_ref), unroll=unroll):
                tiles(*refs, [j * unroll + u for u in range(unroll)], None)
                return carry

            lax.fori_loop(0, nk // unroll, body, 0)

    gate = gate_ref[...].astype(F32)
    if cfg.diff_lam_init is None:
        o = (acc_ref[0] / l_ref[0]).T
    else:
        lam = (jnp.exp(jnp.sum(lam_ref[0:1, :] * lam_ref[1:2, :], axis=-1, keepdims=True))
               - jnp.exp(jnp.sum(lam_ref[2:3, :] * lam_ref[3:4, :], axis=-1, keepdims=True))
               + cfg.diff_lam_init)
        o = (acc_ref[0] / l_ref[0] - lam * (acc_ref[1] / l_ref[1])).T
        o = _rms(o, subg_ref[...], cfg.dv) * (1.0 - cfg.diff_lam_init)
    o_ref[...] = (o * gate).astype(o_ref.dtype)


def _attention(q, srcs, gate, *, heads, kv_group, maps, dqk, dv, scale, sink=None, window=None,
               diff=None):
    assert window is None or sink is not None
    mq = q.shape[0]
    tq = min(mq, ATTN_TQ)
    tk = ATTN_TK
    kx = srcs[0][1] is not None
    cfg = _AttnCfg(maps, dqk, dv, scale, len(srcs), kx, sink is not None, window,
                   None if diff is None else diff[0], tq, tk)
    kw = maps * dqk - (LANES if kx else 0)
    specs = [pl.BlockSpec((tq, maps * dqk), lambda h, i: (i, h))]
    arrs = [q]
    for k, kxa, v in srcs:
        nk = k.shape[0]
        assert nk % tk == 0
        specs.append(pl.BlockSpec((nk, kw), lambda h, i: (0, h // kv_group)))
        arrs.append(k)
        if kx:
            specs.append(pl.BlockSpec((nk, LANES), lambda h, i: (0, 0)))
            arrs.append(kxa)
        specs.append(pl.BlockSpec((1, nk // tk, dv, tk), lambda h, i: (h // kv_group, 0, 0, 0)))
        arrs.append(v)
    if sink is not None:
        specs.append(pl.BlockSpec((1, 8, LANES), lambda h, i: (h, 0, 0)))
        arrs.append(jnp.broadcast_to(sink.astype(F32)[:, None, None], (heads, 8, LANES)))
    specs.append(pl.BlockSpec((tq, dv), lambda h, i: (i, h)))
    arrs.append(gate)
    if diff is not None:
        specs.append(pl.BlockSpec((4, LANES), lambda h, i: (0, 0)))
        arrs.append(diff[1])
        specs.append(pl.BlockSpec((1, dv), lambda h, i: (0, 0)))
        arrs.append(diff[2])
    return pl.pallas_call(
        functools.partial(_attn_kernel, cfg=cfg),
        out_shape=jax.ShapeDtypeStruct((mq, heads * dv), BF16),
        grid=(heads, mq // tq),
        in_specs=specs,
        out_specs=pl.BlockSpec((tq, dv), lambda h, i: (i, h)),
        scratch_shapes=[pltpu.VMEM((maps, 1, tq), F32), pltpu.VMEM((maps, 1, tq), F32),
                        pltpu.VMEM((maps, dv, tq), F32)],
        compiler_params=_params("parallel", "parallel"),
    )(*arrs)


def _conv3_kernel(u_ref, prev_ref, next_ref, w_ref, b_ref, o_ref):
    i = pl.program_id(0)
    tm = u_ref.shape[0]
    x = u_ref[...]
    row = lax.broadcasted_iota(jnp.int32, x.shape, 0)
    prev_row = jnp.where(i > 0, prev_ref[7:8, :], 0.0)
    next_row = jnp.where(i < pl.num_programs(0) - 1, next_ref[0:1, :], 0.0)
    xm = jnp.where(row == 0, prev_row, pltpu.roll(x, 1, 0))
    xp = jnp.where(row == tm - 1, next_row, pltpu.roll(x, tm - 1, 0))
    o_ref[...] = xm * w_ref[0:1, :] + x * w_ref[1:2, :] + xp * w_ref[2:3, :] + b_ref[...]


def _conv3(u, w, b):
    m, n = u.shape
    tm = min(m, 512)
    tn = 512
    nrb = m // 8
    return pl.pallas_call(
        _conv3_kernel,
        out_shape=jax.ShapeDtypeStruct((m, n), F32),
        grid=(m // tm, n // tn),
        in_specs=[pl.BlockSpec((tm, tn), lambda i, j: (i, j)),
                  pl.BlockSpec((8, tn), lambda i, j: (jnp.maximum(i * (tm // 8) - 1, 0), j)),
                  pl.BlockSpec((8, tn), lambda i, j: (jnp.minimum((i + 1) * (tm // 8), nrb - 1), j)),
                  pl.BlockSpec((3, tn), lambda i, j: (0, j)),
                  pl.BlockSpec((1, tn), lambda i, j: (0, j))],
        out_specs=pl.BlockSpec((tm, tn), lambda i, j: (i, j)),
        compiler_params=_params("parallel", "parallel"),
    )(u, u, u, w.astype(F32), _row(b))


def _split_dot(a, b):
    ah = a.astype(BF16)
    al = (a - ah.astype(F32)).astype(BF16)
    bh = b.astype(BF16)
    bl = (b - bh.astype(F32)).astype(BF16)
    d = functools.partial(jnp.dot, preferred_element_type=F32)
    return d(ah, bh) + (d(ah, bl) + d(al, bh))


def _filter_kernel(z_ref, ts_ref, w1_ref, b1_ref, w2_ref, b2_ref, fr_ref, w3_ref, dl_ref, o_ref, ss_ref):
    h = jnp.sin(fr_ref[0:1, :] * (_split_dot(z_ref[...], w1_ref[...]) + b1_ref[...]))
    h = jnp.sin(fr_ref[1:2, :] * (_split_dot(h, w2_ref[...]) + b2_ref[...]))
    hf = _split_dot(h, w3_ref[0])
    kc = hf * jnp.exp(-ts_ref[:, 0:1] * dl_ref[...]) * ts_ref[:, 1:2]
    o_ref[...] = kc

    @pl.when(pl.program_id(0) == 0)
    def _():
        ss_ref[...] = jnp.zeros(ss_ref.shape, F32)

    ss_ref[...] += jnp.sum(kc * kc, axis=0, keepdims=True)


def _filter_tables(L):
    pos = np.arange(L, dtype=np.float32)
    t = (pos / np.float32(max(L - 1, 1))).astype(np.float32)
    bands = np.linspace(1e-4, HYENA_BANDS - 1, HYENA_BANDS, dtype=np.float32)
    ang = (np.float32(2.0 * math.pi / L) * pos[:, None] * bands[None, :]).astype(np.float32)
    z = np.concatenate([t[:, None], np.cos(ang), -np.sin(ang)], axis=-1).astype(np.float32)
    src = np.concatenate([np.arange(L), np.array([0]), np.arange(L - 1, 0, -1)])
    sign = np.concatenate([np.ones(L), np.zeros(1), -np.ones(L - 1)])
    zt = np.zeros((2 * L, LANES), np.float32)
    zt[:, :z.shape[1]] = z[src]
    ts = np.stack([t[src], sign.astype(np.float32)], axis=1)
    return jnp.asarray(zt), jnp.asarray(ts)


def _hyena_filter(L, f_w1, f_b1, f_w2, f_b2, f_freq, f_w3, width):
    zt, ts = _filter_tables(L)
    hid = f_w1.shape[1]
    w1 = jnp.zeros((LANES, hid), F32).at[:f_w1.shape[0]].set(f_w1.astype(F32))
    w3 = f_w3.astype(F32).reshape(hid, 2, 2, width)
    w3 = jnp.stack([w3[:, :, 0].reshape(hid, 2 * width), w3[:, :, 1].reshape(hid, 2 * width)])
    max_decay = math.log(HYENA_DECAY_TARGET) / HYENA_FAST_DECAY
    min_decay = math.log(HYENA_DECAY_TARGET) / HYENA_SLOW_DECAY
    deltas = np.abs(np.linspace(min_decay, max_decay, width, dtype=np.float32))
    dl = jnp.asarray(np.tile(deltas, 2)[None, :], F32)
    tr = min(L, 512)
    nfw = L // tr
    cw = 2 * width
    full = lambda shp: pl.BlockSpec(shp, lambda i: (0,) * len(shp))
    return pl.pallas_call(
        _filter_kernel,
        out_shape=(jax.ShapeDtypeStruct((2 * L, cw), F32), jax.ShapeDtypeStruct((1, cw), F32)),
        grid=(2 * L // tr,),
        in_specs=[pl.BlockSpec((tr, LANES), lambda i: (i, 0)), pl.BlockSpec((tr, 2), lambda i: (i, 0)),
                  full((LANES, hid)), full((1, hid)), full((hid, hid)), full((1, hid)), full((2, hid)),
                  pl.BlockSpec((1, hid, cw), lambda i: (i // nfw, 0, 0)), full((1, cw))],
        out_specs=(pl.BlockSpec((tr, cw), lambda i: (i, 0)), full((1, cw))),
        compiler_params=_params("arbitrary"),
    )(zt, ts, w1, _row(f_b1), f_w2.astype(F32), _row(f_b2), f_freq.astype(F32), w3, dl)


@functools.lru_cache(maxsize=None)
def _dft_mats():
    r = DFT_R
    n = r * r
    k1 = np.arange(r)[:, None]
    n1 = np.arange(r)[None, :]
    base = np.exp(-2j * np.pi * n1 * (k1 + 0.5) / r)
    n2 = np.arange(r)[:, None, None]
    tw = np.exp(-2j * np.pi * n2 * (k1[None] + 0.5) / n)
    ma = tw * base[None]
    ma = np.concatenate([ma.real, ma.imag], axis=1)
    k2 = np.arange(r // 2)[:, None]
    fb = np.exp(-2j * np.pi * np.arange(r)[None, :] * k2 / r)
    fb = np.block([[fb.real, -fb.imag], [fb.imag, fb.real]])
    mc = np.conj(fb[: r // 2, :r] + 1j * fb[r // 2:, :r]).T
    mc = np.block([[mc.real, -mc.imag], [mc.imag, mc.real]])
    md = np.conj(tw * base[None]).transpose(0, 2, 1)[:, : r // 2, :] * (2.0 / n)
    md = np.concatenate([md.real, -md.imag], axis=2)
    return tuple(np.asarray(m, np.float32) for m in (ma, fb, mc, md))


def _dft_a_kernel(m_ref, x_ref, o_ref, *, nb, ch):
    for j in range(nb):
        xs = x_ref[:, j * ch:(j + 1) * ch].astype(BF16)
        o_ref[:, j * ch:(j + 1) * ch] = jnp.dot(m_ref[j], xs, preferred_element_type=F32).astype(BF16)


def _dft_stage_a(x2, ch):
    r = DFT_R
    kdim = x2.shape[0]
    ma = _const_bf(_dft_mats()[0][:, :, :kdim])
    nb = max(1, min(r, 16384 // ch))
    return pl.pallas_call(
        functools.partial(_dft_a_kernel, nb=nb, ch=ch),
        out_shape=jax.ShapeDtypeStruct((2 * r, r * ch), BF16),
        grid=(r // nb,),
        in_specs=[pl.BlockSpec((nb, 2 * r, kdim), lambda i: (i, 0, 0)),
                  pl.BlockSpec((kdim, nb * ch), lambda i: (0, i))],
        out_specs=pl.BlockSpec((2 * r, nb * ch), lambda i: (0, i)),
        compiler_params=_params("parallel"),
    )(ma, x2)


def _dft_b_kernel(fb_ref, a_ref, ss_ref, o_ref, *, kb):
    rn = lax.rsqrt(ss_ref[...] + EPS)
    for j in range(kb):
        op = jnp.concatenate([a_ref[0, j], a_ref[1, j]], axis=0)
        o_ref[j] = jnp.dot(fb_ref[...], op, preferred_element_type=F32) * rn


def _dft_stage_b_spectrum(a2, ss, ch):
    r = DFT_R
    fb = _const_bf(_dft_mats()[1])
    kb = max(1, min(r, 8192 // ch))
    a4 = a2.reshape(2, r, r, ch)
    return pl.pallas_call(
        functools.partial(_dft_b_kernel, kb=kb),
        out_shape=jax.ShapeDtypeStruct((r, r, ch), F32),
        grid=(r // kb,),
        in_specs=[pl.BlockSpec((r, 2 * r), lambda i: (0, 0)),
                  pl.BlockSpec((2, kb, r, ch), lambda i: (0, i, 0, 0)),
                  pl.BlockSpec((1, ch), lambda i: (0, 0))],
        out_specs=pl.BlockSpec((kb, r, ch), lambda i: (i, 0, 0)),
        compiler_params=_params("parallel"),
    )(fb, a4, ss)


def _cmul(x, s):
    h = x.shape[0] // 2
    xr, xi, sr, si = x[:h], x[h:], s[:h], s[h:]
    return jnp.concatenate([xr * sr - xi * si, xr * si + xi * sr], axis=0)


def _dft_bc_kernel(fb_ref, mc_ref, a_ref, s_ref, o_ref, *, kb, ch):
    for j in range(kb):
        op = jnp.concatenate([a_ref[0, j], a_ref[1, j]], axis=0)
        xs = jnp.dot(fb_ref[...], op, preferred_element_type=F32)
        y = _cmul(xs, s_ref[j]).astype(BF16)
        o_ref[:, j * ch:(j + 1) * ch] = jnp.dot(mc_ref[...], y, preferred_element_type=F32).astype(BF16)


def _dft_stage_bc(a2, spec, order, ch):
    r = DFT_R
    fb, mc = (_const_bf(m) for m in _dft_mats()[1:3])
    kb = max(1, min(r, 8192 // ch))
    a4 = a2.reshape(2, r, r, ch)
    return pl.pallas_call(
        functools.partial(_dft_bc_kernel, kb=kb, ch=ch),
        out_shape=jax.ShapeDtypeStruct((2 * r, r * ch), BF16),
        grid=(r // kb,),
        in_specs=[pl.BlockSpec((r, 2 * r), lambda i: (0, 0)),
                  pl.BlockSpec((2 * r, r), lambda i: (0, 0)),
                  pl.BlockSpec((2, kb, r, ch), lambda i: (0, i, 0, 0)),
                  pl.BlockSpec((kb, r, ch), lambda i: (i, 0, order))],
        out_specs=pl.BlockSpec((2 * r, kb * ch), lambda i: (0, i)),
        compiler_params=_params("parallel"),
    )(fb, mc, a4, spec)


def _dft_d_kernel(*refs, nb, ch, gated):
    md_ref, c_ref, u_ref, mul_ref, skip_ref = refs[:5]
    gate_ref = refs[5] if gated else None
    o_ref = refs[-1]
    for j in range(nb):
        sl = slice(j * ch, (j + 1) * ch)
        op = jnp.concatenate([c_ref[0, j], c_ref[1, j]], axis=0)
        y = jnp.dot(md_ref[j], op, preferred_element_type=F32)
        z = mul_ref[:, sl] * (y + u_ref[:, sl] * skip_ref[...])
        if gated:
            z = z * gate_ref[:, sl].astype(F32)
        o_ref[:, sl] = z.astype(o_ref.dtype)


def _dft_stage_d(c2, u2, mul2, skip, gate2, ch):
    r = DFT_R
    md = _const_bf(_dft_mats()[3])
    nb = max(1, min(r, 8192 // ch))
    c4 = c2.reshape(2, r, r, ch)
    blk = pl.BlockSpec((r // 2, nb * ch), lambda i: (0, i))
    specs = [pl.BlockSpec((nb, r // 2, 2 * r), lambda i: (i, 0, 0)),
             pl.BlockSpec((2, nb, r, ch), lambda i: (0, i, 0, 0)),
             blk, blk, pl.BlockSpec((1, ch), lambda i: (0, 0))]
    arrs = [md, c4, u2, mul2, skip]
    if gate2 is not None:
        specs.append(blk)
        arrs.append(gate2)
    return pl.pallas_call(
        functools.partial(_dft_d_kernel, nb=nb, ch=ch, gated=gate2 is not None),
        out_shape=jax.ShapeDtypeStruct((r // 2, r * ch), F32 if gate2 is None else BF16),
        grid=(r // nb,),
        in_specs=specs,
        out_specs=blk,
        compiler_params=_params("parallel"),
    )(*arrs)


def _hyena_long(u, gate, filt, ss, skip, width):
    L = u.shape[0]
    r = DFT_R
    assert 2 * L == r * r
    cw = 2 * width
    spec = _dft_stage_b_spectrum(_dft_stage_a(filt.reshape(r, r * cw), cw), ss, cw)
    lay = lambda t: t.reshape(r // 2, r * width)
    v, x1, x2 = (lay(u[:, k * width:(k + 1) * width]) for k in range(3))
    sk = skip.astype(F32)
    c2 = _dft_stage_bc(_dft_stage_a(v, width), spec, 0, width)
    z = _dft_stage_d(c2, v, x1, sk[0:1], None, width)
    c2 = _dft_stage_bc(_dft_stage_a(z, width), spec, 1, width)
    z = _dft_stage_d(c2, z, x2, sk[1:2], lay(gate), width)
    return z.reshape(L, width)


@functools.lru_cache(maxsize=None)
def _short_dft_mats(L):
    n = 2 * L
    k = np.arange(L)[:, None]
    t = np.arange(n)[None, :]
    f = np.exp(-2j * np.pi * t * (k + 0.5) / n)
    fwd = np.concatenate([f.real, f.imag], axis=0)
    inv = np.conj(f[:, :L]).T * (2.0 / n)
    inv = np.concatenate([inv.real, -inv.imag], axis=1)
    return np.asarray(fwd, np.float32), np.asarray(inv, np.float32)


def _short_spec_kernel(f_ref, k_ref, ss_ref, o_ref):
    o_ref[...] = jnp.dot(f_ref[...], k_ref[...].astype(BF16), preferred_element_type=F32) * lax.rsqrt(ss_ref[...] + EPS)


def _short_conv_kernel(*refs, gated):
    f_ref, i_ref, u_ref, s_ref, mul_ref, skip_ref = refs[:6]
    gate_ref = refs[6] if gated else None
    o_ref = refs[-1]
    u = u_ref[...]
    xs = jnp.dot(f_ref[...], u.astype(BF16), preferred_element_type=F32)
    y = jnp.dot(i_ref[...], _cmul(xs, s_ref[...]).astype(BF16), preferred_element_type=F32)
    z = mul_ref[...] * (y + u * skip_ref[...])
    if gated:
        z = z * gate_ref[...].astype(F32)
    o_ref[...] = z.astype(o_ref.dtype)


def _hyena_short(u, gate, filt, ss, skip, width):
    L = u.shape[0]
    fwd, inv = (_const_bf(m) for m in _short_dft_mats(L))
    cw = 2 * width
    tn = 512
    spec = pl.pallas_call(
        _short_spec_kernel,
        out_shape=jax.ShapeDtypeStruct((2 * L, cw), F32),
        grid=(cw // tn,),
        in_specs=[pl.BlockSpec((2 * L, 2 * L), lambda j: (0, 0)), pl.BlockSpec((2 * L, tn), lambda j: (0, j)),
                  pl.BlockSpec((1, tn), lambda j: (0, j))],
        out_specs=pl.BlockSpec((2 * L, tn), lambda j: (0, j)),
        compiler_params=_params("parallel"),
    )(fwd, filt, ss)

    def conv(uarr, ucol, order, mulcol, gate_arr):
        nt = width // tn
        specs = [pl.BlockSpec((2 * L, L), lambda j: (0, 0)), pl.BlockSpec((L, 2 * L), lambda j: (0, 0)),
                 pl.BlockSpec((L, tn), lambda j: (0, ucol * nt + j)),
                 pl.BlockSpec((2 * L, tn), lambda j: (0, order * nt + j)),
                 pl.BlockSpec((L, tn), lambda j: (0, mulcol * nt + j)),
                 pl.BlockSpec((1, tn), lambda j: (0, order * nt + j))]
        arrs = [fwd[:, :L], inv, uarr, spec, u, skip.astype(F32).reshape(1, cw)]
        if gate_arr is not None:
            specs.append(pl.BlockSpec((L, tn), lambda j: (0, j)))
            arrs.append(gate_arr)
        return pl.pallas_call(
            functools.partial(_short_conv_kernel, gated=gate_arr is not None),
            out_shape=jax.ShapeDtypeStruct((L, width), F32 if gate_arr is None else BF16),
            grid=(nt,),
            in_specs=specs,
            out_specs=pl.BlockSpec((L, tn), lambda j: (0, j)),
            compiler_params=_params("parallel"),
        )(*arrs)

    z = conv(u, 0, 0, 1, None)
    return conv(z, 0, 1, 2, gate)


def _swa_layer(h, hc, w_in, q_g, k_g, sink, need_ctx):
    n, c = h.shape[0], hc.shape[0]
    hq, hk, d = 16, 4, HEAD_DIM
    wq, wk, wv, wg = (_bf(w_in[:, a:b]) for a, b in
                      ((0, hq * d), (hq * d, (hq + hk) * d), ((hq + hk) * d, (hq + 2 * hk) * d),
                       ((hq + 2 * hk) * d, 2 * hq * d + 2 * hk * d)))
    cos, sin = _rope_tables(n, d, d, 0)
    qg, kg = _row(q_g), _row(k_g)
    rope = lambda g: [(g, "const"), (cos, "rows"), (sin, "rows")]
    q = _mm(h, wq, _epi_headnorm_rope, rope(qg))
    k = _mm(h, wk, _epi_headnorm_rope, rope(kg))
    v = _mm_t(h, wv, hk)
    g = _mm(h, wg, _epi_silu)
    ck = _mm(hc, wk, _epi_headnorm, [(kg, "const")])
    cv = _mm_t(hc, wv, hk)
    kw = dict(heads=hq, kv_group=hq // hk, maps=1, dqk=d, dv=d, scale=d ** -0.5, sink=sink)
    o = _attention(q, [(k, None, v), (ck, None, cv)], g, window=SWA_WINDOW, **kw)
    oc = None
    if need_ctx:
        cq = _mm(hc, wq, _epi_headnorm, [(qg, "const")])
        cg = _mm(hc, wg, _epi_silu)
        oc = _attention(cq, [(ck, None, cv)], cg, **kw)
    return o, oc


def _mla_layer(h, hc, w_in, qa_g, kva_g, w_qb, w_kvb, qn_nope_g, qn_pe_g, kn_nope_g, kn_pe_g, need_ctx):
    n, c = h.shape[0], hc.shape[0]
    nh = 16
    qr, kvr, rp, nope = MLA_Q_RANK, MLA_KV_RANK, MLA_ROPE, MLA_NOPE
    w_cq = _bf(w_in[:, :qr])
    w_ckv = _bf(w_in[:, qr:qr + kvr])
    w_kr = _bf(jnp.pad(w_in[:, qr + kvr:qr + kvr + rp], ((0, 0), (0, LANES - rp))))
    w_g = _bf(w_in[:, qr + kvr + rp:])
    wqb = jnp.pad(w_qb.reshape(qr, nh, nope + rp), ((0, 0), (0, 0), (0, 2 * LANES - nope - rp)))
    wqb = _bf(wqb.reshape(qr, nh * 2 * LANES))
    wkv = w_kvb.reshape(kvr, nh, 2 * LANES)
    w_kn, w_v = _bf(wkv[:, :, :nope].reshape(kvr, nh * nope)), _bf(wkv[:, :, nope:].reshape(kvr, nh * LANES))
    pad64 = lambda g: jnp.pad(_row(g), ((0, 0), (0, LANES - rp)))
    gqp, gkp = pad64(qn_pe_g), pad64(kn_pe_g)
    cos, sin = _rope_tables(n, rp, LANES, 0)
    cos_c = jnp.pad(jnp.ones((c, rp), F32), ((0, 0), (0, LANES - rp)))
    sin_c = jnp.zeros((c, LANES), F32)

    def side(hh, cs, sn, queries):
        ckv = _mm(hh, w_ckv, _epi_rownorm, [(_row(kva_g), "row")], tn=kvr)
        kp = _mm(hh, w_kr, _epi_mla_pe, [(gkp, "const"), (cs, "rows"), (sn, "rows")])
        kn = _mm(ckv, w_kn, _epi_headnorm, [(_row(kn_nope_g), "const")])
        v = _mm_t(ckv, w_v, nh)
        if not queries:
            return kn, kp, v, None, None
        cq = _mm(hh, w_cq, _epi_rownorm, [(_row(qa_g), "row")], tn=qr)
        q = _mm(cq, wqb, _epi_mla_q, [(_row(qn_nope_g), "const"), (gqp, "const"), (cs, "rows"), (sn, "rows")])
        g = _mm(hh, w_g, _epi_silu)
        return kn, kp, v, q, g

    kn, kp, v, q, g = side(h, cos, sin, True)
    ckn, ckp, cv, cq, cg = side(hc, cos_c, sin_c, need_ctx)
    kw = dict(heads=nh, kv_group=1, maps=1, dqk=2 * LANES, dv=LANES, scale=(nope + rp) ** -0.5)
    o = _attention(q, [(kn, kp, v), (ckn, ckp, cv)], g, **kw)
    oc = _attention(cq, [(ckn, ckp, cv)], cg, **kw) if need_ctx else None
    return o, oc


def _hyena_layer(h, hc, w_in, conv_w, conv_b, f_w1, f_b1, f_w2, f_b2, f_freq, f_w3, skip, need_ctx):
    width = skip.shape[1]
    w_u, w_g = _bf(w_in[:, :3 * width]), _bf(w_in[:, 3 * width:])

    def branch(hh, long):
        L = hh.shape[0]
        u = _conv3(_mm(hh, w_u, _epi_plain, out_dtype=F32), conv_w, conv_b)
        g = _mm(hh, w_g, _epi_silu)
        filt, ss = _hyena_filter(L, f_w1, f_b1, f_w2, f_b2, f_freq, f_w3, width)
        return (_hyena_long if long else _hyena_short)(u, g, filt, ss, skip, width)

    o = branch(h, True)
    oc = branch(hc, False) if need_ctx else None
    return o, oc


def _diff_layer(h, hc, w_in, q_g, k_g, lq1, lk1, lq2, lk2, subln_g, lam_init, need_ctx):
    n, c = h.shape[0], hc.shape[0]
    nh, d = 8, HEAD_DIM
    w = nh * 2 * d
    wq, wk, wv, wg = (_bf(w_in[:, i * w:(i + 1) * w]) for i in range(4))
    cos, sin = _rope_tables(n, d, d, 0)
    qg, kg = _row(q_g), _row(k_g)
    rope = lambda g: [(g, "const"), (cos, "rows"), (sin, "rows")]
    q = _mm(h, wq, _epi_headnorm_rope, rope(qg))
    k = _mm(h, wk, _epi_headnorm_rope, rope(kg))
    v = _mm_t(h, wv, nh)
    g = _mm(h, wg, _epi_silu)
    ck = _mm(hc, wk, _epi_headnorm, [(kg, "const")])
    cv = _mm_t(hc, wv, nh)
    lam_vecs = jnp.stack([lq1, lk1, lq2, lk2]).astype(F32)
    kw = dict(heads=nh, kv_group=1, maps=2, dqk=d, dv=2 * d, scale=d ** -0.5,
              diff=(lam_init, lam_vecs, _row(subln_g)))
    o = _attention(q, [(k, None, v), (ck, None, cv)], g, **kw)
    oc = None
    if need_ctx:
        cq = _mm(hc, wq, _epi_headnorm, [(qg, "const")])
        cg = _mm(hc, wg, _epi_silu)
        oc = _attention(cq, [(ck, None, cv)], cg, **kw)
    return o, oc


def kernel(x, c, ctx, c_ctx, norm_g, ada_w, ada_b, swa_w_in, swa_q_g, swa_k_g, swa_sink, swa_w_out, mla_w_in, mla_qa_g, mla_kva_g, mla_w_qb, mla_w_kvb, mla_qn_nope_g, mla_qn_pe_g, mla_kn_nope_g, mla_kn_pe_g, mla_w_out, hyena_w_in, hyena_conv_w, hyena_conv_b, hyena_f_w1, hyena_f_b1, hyena_f_w2, hyena_f_b2, hyena_f_freq, hyena_f_w3, hyena_skip, hyena_w_out, diff_w_in, diff_q_g, diff_k_g, diff_lq1, diff_lk1, diff_lq2, diff_lk2, diff_subln_g, diff_w_out):
    depth, d = norm_g.shape
    assert x.shape[0] == 1
    xs, cs = x[0], ctx[0]
    mods = _adaln(c, c_ctx, ada_w, ada_b)
    for i in range(depth):
        kind, j = i % 4, i // 4
        need_ctx = i < depth - 1
        shift, scale, gate = (mods[i, 0, a * d:(a + 1) * d] for a in range(3))
        shift_c, scale_c, gate_c = (mods[i, 1, a * d:(a + 1) * d] for a in range(3))
        h = _modnorm(xs, norm_g[i], scale, shift)
        hc = _modnorm(cs, norm_g[i], scale_c, shift_c)
        if kind == 0:
            o, oc = _swa_layer(h, hc, swa_w_in[j], swa_q_g[j], swa_k_g[j], swa_sink[j], need_ctx)
            w_out = swa_w_out[j]
        elif kind == 1:
            o, oc = _mla_layer(h, hc, mla_w_in[j], mla_qa_g[j], mla_kva_g[j], mla_w_qb[j], mla_w_kvb[j],
                               mla_qn_nope_g[j], mla_qn_pe_g[j], mla_kn_nope_g[j], mla_kn_pe_g[j], need_ctx)
            w_out = mla_w_out[j]
        elif kind == 2:
            o, oc = _hyena_layer(h, hc, hyena_w_in[j], hyena_conv_w[j], hyena_conv_b[j], hyena_f_w1[j],
                                 hyena_f_b1[j], hyena_f_w2[j], hyena_f_b2[j], hyena_f_freq[j], hyena_f_w3[j],
                                 hyena_skip[j], need_ctx)
            w_out = hyena_w_out[j]
        else:
            lam_init = 0.8 - 0.6 * math.exp(-0.3 * i)
            o, oc = _diff_layer(h, hc, diff_w_in[j], diff_q_g[j], diff_k_g[j], diff_lq1[j], diff_lk1[j],
                                diff_lq2[j], diff_lk2[j], diff_subln_g[j], lam_init, need_ctx)
            w_out = diff_w_out[j]
        wo = _bf(w_out)
        xs = _mm(o, wo, _epi_resid, [(xs, "tile"), (_row(gate), "row")], out_dtype=F32)
        if need_ctx:
            cs = _mm(oc, wo, _epi_resid, [(cs, "tile"), (_row(gate_c), "row")], out_dtype=F32)
    return xs[None]
```

```python
import functools
import math

import numpy as np
import jax
import jax.numpy as jnp
from jax import lax
from jax.experimental import pallas as pl
from jax.experimental.pallas import tpu as pltpu

F32 = jnp.float32
BF16 = jnp.bfloat16

EPS = 1e-6
NEG_INF = -1e30
LOG2E = 1.4426950408889634
LANES = 128
VMEM_LIMIT_BYTES = 56 * 1024 * 1024

GRID_W = 64
ROPE_BASE = 10000.0
HEAD_DIM = 128
SWA_WINDOW = 128
MLA_Q_RANK, MLA_KV_RANK, MLA_NOPE, MLA_ROPE = 512, 256, 128, 64
HYENA_BANDS = 16
HYENA_DECAY_TARGET, HYENA_FAST_DECAY, HYENA_SLOW_DECAY = 1e-2, 0.3, 1.5
DFT_R = 128
ATTN_TK = 256
ATTN_TQ = 512


def _params(*sem):
    return pltpu.CompilerParams(dimension_semantics=sem, vmem_limit_bytes=VMEM_LIMIT_BYTES)


def _row(v):
    return v.reshape(1, -1).astype(F32)


def _bf(w):
    return w.astype(BF16)


def _const_bf(m):
    return jnp.asarray(m, F32).astype(BF16)


def _adaln_kernel(c_ref, w_ref, b_ref, o_ref):
    cv = c_ref[...]
    cv = cv * jax.nn.sigmoid(cv)
    w = w_ref[0]
    r0 = jnp.sum(w * cv[:, 0:1], axis=0, keepdims=True)
    r1 = jnp.sum(w * cv[:, 1:2], axis=0, keepdims=True)
    o_ref[0] = jnp.concatenate([r0, r1], axis=0) + b_ref[0]


def _adaln(c, c_ctx, ada_w, ada_b):
    depth, d, n3 = ada_w.shape
    tn = 512
    cc = jnp.stack([c.reshape(d), c_ctx.reshape(d)], axis=1).astype(F32)
    return pl.pallas_call(
        _adaln_kernel,
        out_shape=jax.ShapeDtypeStruct((depth, 2, n3), F32),
        grid=(depth, n3 // tn),
        in_specs=[pl.BlockSpec((d, 2), lambda l, j: (0, 0)),
                  pl.BlockSpec((1, d, tn), lambda l, j: (l, 0, j)),
                  pl.BlockSpec((1, 1, tn), lambda l, j: (l, 0, j))],
        out_specs=pl.BlockSpec((1, 2, tn), lambda l, j: (l, 0, j)),
        compiler_params=_params("parallel", "parallel"),
    )(cc, ada_w, ada_b.reshape(depth, 1, n3))


def _modnorm_kernel(x_ref, g_ref, sc_ref, sh_ref, o_ref):
    x = x_ref[...]
    y = x * lax.rsqrt(jnp.mean(x * x, axis=-1, keepdims=True) + EPS) * g_ref[...]
    o_ref[...] = (y * (1.0 + sc_ref[...]) + sh_ref[...]).astype(o_ref.dtype)


def _modnorm(x, g, scale, shift):
    m, d = x.shape
    tm = min(m, 512)
    vec = pl.BlockSpec((1, d), lambda i: (0, 0))
    return pl.pallas_call(
        _modnorm_kernel,
        out_shape=jax.ShapeDtypeStruct((m, d), BF16),
        grid=(m // tm,),
        in_specs=[pl.BlockSpec((tm, d), lambda i: (i, 0)), vec, vec, vec],
        out_specs=pl.BlockSpec((tm, d), lambda i: (i, 0)),
        compiler_params=_params("parallel"),
    )(x, _row(g), _row(scale), _row(shift))


def _rms(y, g, n_real):
    ss = jnp.sum(y * y, axis=-1, keepdims=True) * (1.0 / n_real)
    return y * lax.rsqrt(ss + EPS) * g


def _rope(y, cos, sin_signed, shift):
    up = pltpu.roll(y, LANES - shift, 1)
    dn = pltpu.roll(y, shift, 1)
    lane = lax.broadcasted_iota(jnp.int32, y.shape, 1)
    rot = jnp.where((lane & (2 * shift - 1)) < shift, up, dn)
    return y * cos + rot * sin_signed


def _epi_plain(acc, o_ref):
    o_ref[...] = acc.astype(o_ref.dtype)


def _epi_silu(acc, o_ref):
    o_ref[...] = (acc * jax.nn.sigmoid(acc)).astype(o_ref.dtype)


def _epi_headnorm(acc, g_ref, o_ref):
    for h in range(acc.shape[1] // LANES):
        sl = slice(h * LANES, (h + 1) * LANES)
        o_ref[:, sl] = _rms(acc[:, sl], g_ref[...], LANES).astype(o_ref.dtype)


def _epi_headnorm_rope(acc, g_ref, cos_ref, sin_ref, o_ref):
    for h in range(acc.shape[1] // LANES):
        sl = slice(h * LANES, (h + 1) * LANES)
        y = _rms(acc[:, sl], g_ref[...], LANES)
        o_ref[:, sl] = _rope(y, cos_ref[...], sin_ref[...], HEAD_DIM // 4).astype(o_ref.dtype)


def _epi_rownorm(acc, g_ref, o_ref):
    o_ref[...] = _rms(acc, g_ref[...], acc.shape[1]).astype(o_ref.dtype)


def _epi_mla_pe(acc, g_ref, cos_ref, sin_ref, o_ref):
    y = _rms(acc, g_ref[...], MLA_ROPE)
    o_ref[...] = _rope(y, cos_ref[...], sin_ref[...], MLA_ROPE // 4).astype(o_ref.dtype)


def _epi_mla_q(acc, gn_ref, gp_ref, cos_ref, sin_ref, o_ref):
    for h in range(acc.shape[1] // (2 * LANES)):
        a = slice(2 * h * LANES, (2 * h + 1) * LANES)
        b = slice((2 * h + 1) * LANES, (2 * h + 2) * LANES)
        o_ref[:, a] = _rms(acc[:, a], gn_ref[...], MLA_NOPE).astype(o_ref.dtype)
        y = _rms(acc[:, b], gp_ref[...], MLA_ROPE)
        o_ref[:, b] = _rope(y, cos_ref[...], sin_ref[...], MLA_ROPE // 4).astype(o_ref.dtype)


def _epi_resid(acc, x_ref, gate_ref, o_ref):
    o_ref[...] = x_ref[...] + gate_ref[...] * acc


def _mm_kernel(*refs, epi):
    a_ref, w_ref = refs[0], refs[1]
    acc = jnp.dot(a_ref[...], w_ref[...], preferred_element_type=F32)
    epi(acc, *refs[2:])


def _mm(a, w, epi, extras=(), out_dtype=BF16, tn=512):
    m, k = a.shape
    n = w.shape[1]
    tm = min(m, 1024)
    tn = min(tn, n)
    assert m % tm == 0 and n % tn == 0
    specs = [pl.BlockSpec((tm, k), lambda i, j: (i, 0)), pl.BlockSpec((k, tn), lambda i, j: (0, j))]
    arrs = [a, w]
    for arr, kind in extras:
        if kind == "row":
            specs.append(pl.BlockSpec((1, tn), lambda i, j: (0, j)))
        elif kind == "tile":
            specs.append(pl.BlockSpec((tm, tn), lambda i, j: (i, j)))
        elif kind == "rows":
            specs.append(pl.BlockSpec((tm, arr.shape[1]), lambda i, j: (i, 0)))
        else:
            specs.append(pl.BlockSpec(arr.shape, lambda i, j: (0, 0)))
        arrs.append(arr)
    return pl.pallas_call(
        functools.partial(_mm_kernel, epi=epi),
        out_shape=jax.ShapeDtypeStruct((m, n), out_dtype),
        grid=(m // tm, n // tn),
        in_specs=specs,
        out_specs=pl.BlockSpec((tm, tn), lambda i, j: (i, j)),
        compiler_params=_params("parallel", "parallel"),
    )(*arrs)


def _mm_t_kernel(wt_ref, a_ref, o_ref):
    acc = lax.dot_general(wt_ref[...], a_ref[...], (((1,), (1,)), ((), ())), preferred_element_type=F32)
    for t in range(o_ref.shape[1]):
        o_ref[0, t] = acc[:, t * ATTN_TK:(t + 1) * ATTN_TK].astype(o_ref.dtype)


def _mm_t(a, w, heads):
    m, k = a.shape
    dv = w.shape[1] // heads
    tm = min(m, 1024)
    return pl.pallas_call(
        _mm_t_kernel,
        out_shape=jax.ShapeDtypeStruct((heads, m // ATTN_TK, dv, ATTN_TK), BF16),
        grid=(m // tm, heads),
        in_specs=[pl.BlockSpec((dv, k), lambda i, j: (j, 0)), pl.BlockSpec((tm, k), lambda i, j: (i, 0))],
        out_specs=pl.BlockSpec((1, tm // ATTN_TK, dv, ATTN_TK), lambda i, j: (j, i, 0, 0)),
        compiler_params=_params("parallel", "parallel"),
    )(w.T, a)


def _rope_tables(n_tok, rot_dim, pad_to, identity_rows):
    half = rot_dim // 2
    qr = half // 2
    rows = n_tok // GRID_W
    row = np.repeat(np.arange(rows, dtype=np.float64), GRID_W)
    col = np.tile(np.arange(GRID_W, dtype=np.float64), rows)
    inv = 1.0 / (ROPE_BASE ** (np.arange(0, half, 2, dtype=np.float64) / half))
    ar = row[:, None] * inv[None, :]
    ac = col[:, None] * inv[None, :]
    ang = np.concatenate([ar, ar, ac, ac], axis=-1)
    sign = np.where((np.arange(rot_dim) % (2 * qr)) < qr, -1.0, 1.0)
    cos = np.cos(ang)
    sin = np.sin(ang) * sign[None, :]
    cos = np.concatenate([cos, np.ones((identity_rows, rot_dim))], axis=0)
    sin = np.concatenate([sin, np.zeros((identity_rows, rot_dim))], axis=0)
    pad = ((0, 0), (0, pad_to - rot_dim))
    return (jnp.asarray(np.pad(cos, pad), F32), jnp.asarray(np.pad(sin, pad), F32))


class _AttnCfg:
    def __init__(self, maps, dqk, dv, n_src, kx, sink, window, diff_lam_init, tq, tk):
        self.maps, self.dqk, self.dv = maps, dqk, dv
        self.n_src, self.kx, self.sink, self.window = n_src, kx, sink, window
        self.diff_lam_init, self.tq, self.tk = diff_lam_init, tq, tk


def _attn_kernel(*refs, cfg):
    it = iter(refs)
    q_ref = next(it)
    srcs = []
    for _ in range(cfg.n_src):
        k_ref = next(it)
        kx_ref = next(it) if cfg.kx else None
        v_ref = next(it)
        srcs.append((k_ref, kx_ref, v_ref))
    sink_ref = next(it) if cfg.sink else None
    gate_ref = next(it)
    if cfg.diff_lam_init is not None:
        lam_ref, subg_ref = next(it), next(it)
    o_ref = next(it)
    m_ref, l_ref, acc_ref = next(it), next(it), next(it)

    tq, tk, maps, dqk = cfg.tq, cfg.tk, cfg.maps, cfg.dqk
    qi = pl.program_id(1)
    dv = cfg.dv
    ones_rows = jnp.ones((16, tk), BF16)

    if cfg.sink:
        m_ref[...] = jnp.full(m_ref.shape, 1.0, F32) * (sink_ref[0, 0:1, 0:1] * LOG2E)
        l_ref[...] = jnp.ones(l_ref.shape, F32)
    else:
        m_ref[...] = jnp.full(m_ref.shape, NEG_INF, F32)
        l_ref[...] = jnp.zeros(l_ref.shape, F32)
    acc_ref[...] = jnp.zeros(acc_ref.shape, F32)

    def tiles(k_ref, kx_ref, v_ref, kts, band):
        m = [m_ref[mp] for mp in range(maps)]
        l = [l_ref[mp] for mp in range(maps)]
        acc = [acc_ref[mp] for mp in range(maps)]
        def scores(u):
            koff = pl.multiple_of(kts[u] * tk, tk)
            out = []
            for mp in range(maps):
                q = q_ref[:, mp * dqk:(mp + 1) * dqk]
                if kx_ref is None:
                    k = k_ref[pl.ds(koff, tk), mp * dqk:(mp + 1) * dqk]
                else:
                    k = jnp.concatenate([k_ref[pl.ds(koff, tk), :], kx_ref[pl.ds(koff, tk), :]], axis=1)
                out.append(lax.dot_general(k, q, (((1,), (1,)), ((), ())), preferred_element_type=F32))
            return out

        s_next = scores(0)
        for u, kt in enumerate(kts):
            s_cur = s_next
            if u + 1 < len(kts):
                s_next = scores(u + 1)
            vt = jnp.concatenate([v_ref[0, kt], ones_rows], axis=0)
            for mp in range(maps):
                s = s_cur[mp]
                if band is not None:
                    kpos = band[u][0] * tk + lax.broadcasted_iota(jnp.int32, (tk, tq), 0)
                    qpos = qi * tq + lax.broadcasted_iota(jnp.int32, (tk, tq), 1)
                    dlt = kpos - qpos
                    s = jnp.where((dlt >= -band[u][1]) & (dlt <= band[u][1]), s, NEG_INF)
                m_new = jnp.maximum(m[mp], jnp.max(s, axis=0, keepdims=True))
                alpha = jnp.exp2(m[mp] - m_new)
                p = jnp.exp2(s - m_new)
                pv = jnp.dot(vt, p.astype(BF16), preferred_element_type=F32)
                l[mp] = alpha * l[mp] + pv[dv:dv + 1]
                acc[mp] = alpha * acc[mp] + pv[:dv]
                m[mp] = m_new
        for mp in range(maps):
            m_ref[mp], l_ref[mp], acc_ref[mp] = m[mp], l[mp], acc[mp]

    for si, (k_ref, kx_ref, v_ref) in enumerate(srcs):
        nk = k_ref.shape[0] // tk
        if cfg.window is not None and si == 0:
            kts, band = [], []
            for u in range(tq // tk + 2):
                raw = qi * (tq // tk) - 1 + u
                inside = (raw >= 0) & (raw < nk)
                kts.append(jnp.clip(raw, 0, nk - 1))
                band.append((raw, jnp.where(inside, cfg.window, -1)))
            tiles(k_ref, kx_ref, v_ref, kts, band)
        else:
            unroll = max(u for u in (8 // maps, 4 // maps, 2, 1) if nk % u == 0)

            def body(j, carry, refs=(k_ref, kx_ref, v_ref), unroll=unroll):
                tiles(*refs, [j * unroll + u for u in range(unroll)], None)
                return carry

            lax.fori_loop(0, nk // unroll, body, 0)

    gate = gate_ref[...].astype(F32)
    if cfg.diff_lam_init is None:
        o = (acc_ref[0] / l_ref[0]).T
    else:
        lam = (jnp.exp(jnp.sum(lam_ref[0:1, :] * lam_ref[1:2, :], axis=-1, keepdims=True))
               - jnp.exp(jnp.sum(lam_ref[2:3, :] * lam_ref[3:4, :], axis=-1, keepdims=True))
               + cfg.diff_lam_init)
        o = (acc_ref[0] / l_ref[0] - lam * (acc_ref[1] / l_ref[1])).T
        o = _rms(o, subg_ref[...], cfg.dv) * (1.0 - cfg.diff_lam_init)
    o_ref[...] = (o * gate).astype(o_ref.dtype)


def _attention(q, srcs, gate, *, heads, kv_group, maps, dqk, dv, sink=None, window=None, diff=None):
    assert window is None or sink is not None
    mq = q.shape[0]
    tq = min(mq, ATTN_TQ)
    tk = ATTN_TK
    kx = srcs[0][1] is not None
    cfg = _AttnCfg(maps, dqk, dv, len(srcs), kx, sink is not None, window,
                   None if diff is None else diff[0], tq, tk)
    kw = maps * dqk - (LANES if kx else 0)
    specs = [pl.BlockSpec((tq, maps * dqk), lambda h, i: (i, h))]
    arrs = [q]
    for k, kxa, v in srcs:
        nk = k.shape[0]
        assert nk % tk == 0
        specs.append(pl.BlockSpec((nk, kw), lambda h, i: (0, h // kv_group)))
        arrs.append(k)
        if kx:
            specs.append(pl.BlockSpec((nk, LANES), lambda h, i: (0, 0)))
            arrs.append(kxa)
        specs.append(pl.BlockSpec((1, nk // tk, dv, tk), lambda h, i: (h // kv_group, 0, 0, 0)))
        arrs.append(v)
    if sink is not None:
        specs.append(pl.BlockSpec((1, 8, LANES), lambda h, i: (h, 0, 0)))
        arrs.append(jnp.broadcast_to(sink.astype(F32)[:, None, None], (heads, 8, LANES)))
    specs.append(pl.BlockSpec((tq, dv), lambda h, i: (i, h)))
    arrs.append(gate)
    if diff is not None:
        specs.append(pl.BlockSpec((4, LANES), lambda h, i: (0, 0)))
        arrs.append(diff[1])
        specs.append(pl.BlockSpec((1, dv), lambda h, i: (0, 0)))
        arrs.append(diff[2])
    return pl.pallas_call(
        functools.partial(_attn_kernel, cfg=cfg),
        out_shape=jax.ShapeDtypeStruct((mq, heads * dv), BF16),
        grid=(heads, mq // tq),
        in_specs=specs,
        out_specs=pl.BlockSpec((tq, dv), lambda h, i: (i, h)),
        scratch_shapes=[pltpu.VMEM((maps, 1, tq), F32), pltpu.VMEM((maps, 1, tq), F32),
                        pltpu.VMEM((maps, dv, tq), F32)],
        compiler_params=_params("parallel", "parallel"),
    )(*arrs)


def _conv3_kernel(u_ref, prev_ref, next_ref, w_ref, b_ref, o_ref):
    i = pl.program_id(0)
    tm = u_ref.shape[0]
    x = u_ref[...]
    row = lax.broadcasted_iota(jnp.int32, x.shape, 0)
    prev_row = jnp.where(i > 0, prev_ref[7:8, :], 0.0)
    next_row = jnp.where(i < pl.num_programs(0) - 1, next_ref[0:1, :], 0.0)
    xm = jnp.where(row == 0, prev_row, pltpu.roll(x, 1, 0))
    xp = jnp.where(row == tm - 1, next_row, pltpu.roll(x, tm - 1, 0))
    o_ref[...] = xm * w_ref[0:1, :] + x * w_ref[1:2, :] + xp * w_ref[2:3, :] + b_ref[...]


def _conv3(u, w, b):
    m, n = u.shape
    tm = min(m, 512)
    tn = 512
    nrb = m // 8
    return pl.pallas_call(
        _conv3_kernel,
        out_shape=jax.ShapeDtypeStruct((m, n), F32),
        grid=(m // tm, n // tn),
        in_specs=[pl.BlockSpec((tm, tn), lambda i, j: (i, j)),
                  pl.BlockSpec((8, tn), lambda i, j: (jnp.maximum(i * (tm // 8) - 1, 0), j)),
                  pl.BlockSpec((8, tn), lambda i, j: (jnp.minimum((i + 1) * (tm // 8), nrb - 1), j)),
                  pl.BlockSpec((3, tn), lambda i, j: (0, j)),
                  pl.BlockSpec((1, tn), lambda i, j: (0, j))],
        out_specs=pl.BlockSpec((tm, tn), lambda i, j: (i, j)),
        compiler_params=_params("parallel", "parallel"),
    )(u, u, u, w.astype(F32), _row(b))


def _split_dot(a, b):
    ah = a.astype(BF16)
    al = (a - ah.astype(F32)).astype(BF16)
    bh = b.astype(BF16)
    bl = (b - bh.astype(F32)).astype(BF16)
    d = functools.partial(jnp.dot, preferred_element_type=F32)
    return d(ah, bh) + (d(ah, bl) + d(al, bh))


def _filter_kernel(z_ref, ts_ref, w1_ref, b1_ref, w2_ref, b2_ref, fr_ref, w3_ref, dl_ref, o_ref, ss_ref):
    h = jnp.sin(fr_ref[0:1, :] * (_split_dot(z_ref[...], w1_ref[...]) + b1_ref[...]))
    h = jnp.sin(fr_ref[1:2, :] * (_split_dot(h, w2_ref[...]) + b2_ref[...]))
    hf = _split_dot(h, w3_ref[0])
    kc = hf * jnp.exp(-ts_ref[:, 0:1] * dl_ref[...]) * ts_ref[:, 1:2]
    o_ref[...] = kc

    @pl.when(pl.program_id(0) == 0)
    def _():
        ss_ref[...] = jnp.zeros(ss_ref.shape, F32)

    ss_ref[...] += jnp.sum(kc * kc, axis=0, keepdims=True)


def _filter_tables(L):
    pos = np.arange(L, dtype=np.float32)
    t = (pos / np.float32(max(L - 1, 1))).astype(np.float32)
    bands = np.linspace(1e-4, HYENA_BANDS - 1, HYENA_BANDS, dtype=np.float32)
    ang = (np.float32(2.0 * math.pi / L) * pos[:, None] * bands[None, :]).astype(np.float32)
    z = np.concatenate([t[:, None], np.cos(ang), -np.sin(ang)], axis=-1).astype(np.float32)
    src = np.concatenate([np.arange(L), np.array([0]), np.arange(L - 1, 0, -1)])
    sign = np.concatenate([np.ones(L), np.zeros(1), -np.ones(L - 1)])
    zt = np.zeros((2 * L, LANES), np.float32)
    zt[:, :z.shape[1]] = z[src]
    ts = np.stack([t[src], sign.astype(np.float32)], axis=1)
    return jnp.asarray(zt), jnp.asarray(ts)


def _hyena_filter(L, f_w1, f_b1, f_w2, f_b2, f_freq, f_w3, width):
    zt, ts = _filter_tables(L)
    hid = f_w1.shape[1]
    w1 = jnp.zeros((LANES, hid), F32).at[:f_w1.shape[0]].set(f_w1.astype(F32))
    w3 = f_w3.astype(F32).reshape(hid, 2, 2, width)
    w3 = jnp.stack([w3[:, :, 0].reshape(hid, 2 * width), w3[:, :, 1].reshape(hid, 2 * width)])
    max_decay = math.log(HYENA_DECAY_TARGET) / HYENA_FAST_DECAY
    min_decay = math.log(HYENA_DECAY_TARGET) / HYENA_SLOW_DECAY
    deltas = np.abs(np.linspace(min_decay, max_decay, width, dtype=np.float32))
    dl = jnp.asarray(np.tile(deltas, 2)[None, :], F32)
    tr = min(L, 512)
    nfw = L // tr
    cw = 2 * width
    full = lambda shp: pl.BlockSpec(shp, lambda i: (0,) * len(shp))
    return pl.pallas_call(
        _filter_kernel,
        out_shape=(jax.ShapeDtypeStruct((2 * L, cw), F32), jax.ShapeDtypeStruct((1, cw), F32)),
        grid=(2 * L // tr,),
        in_specs=[pl.BlockSpec((tr, LANES), lambda i: (i, 0)), pl.BlockSpec((tr, 2), lambda i: (i, 0)),
                  full((LANES, hid)), full((1, hid)), full((hid, hid)), full((1, hid)), full((2, hid)),
                  pl.BlockSpec((1, hid, cw), lambda i: (i // nfw, 0, 0)), full((1, cw))],
        out_specs=(pl.BlockSpec((tr, cw), lambda i: (i, 0)), full((1, cw))),
        compiler_params=_params("arbitrary"),
    )(zt, ts, w1, _row(f_b1), f_w2.astype(F32), _row(f_b2), f_freq.astype(F32), w3, dl)


@functools.lru_cache(maxsize=None)
def _dft_mats():
    r = DFT_R
    n = r * r
    k1 = np.arange(r)[:, None]
    n1 = np.arange(r)[None, :]
    base = np.exp(-2j * np.pi * n1 * (k1 + 0.5) / r)
    n2 = np.arange(r)[:, None, None]
    tw = np.exp(-2j * np.pi * n2 * (k1[None] + 0.5) / n)
    ma = tw * base[None]
    ma = np.concatenate([ma.real, ma.imag], axis=1)
    k2 = np.arange(r // 2)[:, None]
    fb = np.exp(-2j * np.pi * np.arange(r)[None, :] * k2 / r)
    fb = np.block([[fb.real, -fb.imag], [fb.imag, fb.real]])
    mc = np.conj(fb[: r // 2, :r] + 1j * fb[r // 2:, :r]).T
    mc = np.block([[mc.real, -mc.imag], [mc.imag, mc.real]])
    md = np.conj(tw * base[None]).transpose(0, 2, 1)[:, : r // 2, :] * (2.0 / n)
    md = np.concatenate([md.real, -md.imag], axis=2)
    return tuple(np.asarray(m, np.float32) for m in (ma, fb, mc, md))


def _dft_a_kernel(m_ref, x_ref, o_ref, *, nb, ch):
    for j in range(nb):
        xs = x_ref[:, j * ch:(j + 1) * ch].astype(BF16)
        o_ref[:, j * ch:(j + 1) * ch] = jnp.dot(m_ref[j], xs, preferred_element_type=F32).astype(BF16)


def _dft_stage_a(x2, ch):
    r = DFT_R
    kdim = x2.shape[0]
    ma = _const_bf(_dft_mats()[0][:, :, :kdim])
    nb = max(1, min(r, 16384 // ch))
    return pl.pallas_call(
        functools.partial(_dft_a_kernel, nb=nb, ch=ch),
        out_shape=jax.ShapeDtypeStruct((2 * r, r * ch), BF16),
        grid=(r // nb,),
        in_specs=[pl.BlockSpec((nb, 2 * r, kdim), lambda i: (i, 0, 0)),
                  pl.BlockSpec((kdim, nb * ch), lambda i: (0, i))],
        out_specs=pl.BlockSpec((2 * r, nb * ch), lambda i: (0, i)),
        compiler_params=_params("parallel"),
    )(ma, x2)


def _dft_b_kernel(fb_ref, a_ref, ss_ref, o_ref, *, kb):
    rn = lax.rsqrt(ss_ref[...] + EPS)
    for j in range(kb):
        op = jnp.concatenate([a_ref[0, j], a_ref[1, j]], axis=0)
        o_ref[j] = jnp.dot(fb_ref[...], op, preferred_element_type=F32) * rn


def _dft_stage_b_spectrum(a2, ss, ch):
    r = DFT_R
    fb = _const_bf(_dft_mats()[1])
    kb = max(1, min(r, 8192 // ch))
    a4 = a2.reshape(2, r, r, ch)
    return pl.pallas_call(
        functools.partial(_dft_b_kernel, kb=kb),
        out_shape=jax.ShapeDtypeStruct((r, r, ch), F32),
        grid=(r // kb,),
        in_specs=[pl.BlockSpec((r, 2 * r), lambda i: (0, 0)),
                  pl.BlockSpec((2, kb, r, ch), lambda i: (0, i, 0, 0)),
                  pl.BlockSpec((1, ch), lambda i: (0, 0))],
        out_specs=pl.BlockSpec((kb, r, ch), lambda i: (i, 0, 0)),
        compiler_params=_params("parallel"),
    )(fb, a4, ss)


def _cmul(x, s):
    h = x.shape[0] // 2
    xr, xi, sr, si = x[:h], x[h:], s[:h], s[h:]
    return jnp.concatenate([xr * sr - xi * si, xr * si + xi * sr], axis=0)


def _dft_bc_kernel(fb_ref, mc_ref, a_ref, s_ref, o_ref, *, kb, ch):
    for j in range(kb):
        op = jnp.concatenate([a_ref[0, j], a_ref[1, j]], axis=0)
        xs = jnp.dot(fb_ref[...], op, preferred_element_type=F32)
        y = _cmul(xs, s_ref[j]).astype(BF16)
        o_ref[:, j * ch:(j + 1) * ch] = jnp.dot(mc_ref[...], y, preferred_element_type=F32).astype(BF16)


def _dft_stage_bc(a2, spec, order, ch):
    r = DFT_R
    fb, mc = (_const_bf(m) for m in _dft_mats()[1:3])
    kb = max(1, min(r, 8192 // ch))
    a4 = a2.reshape(2, r, r, ch)
    return pl.pallas_call(
        functools.partial(_dft_bc_kernel, kb=kb, ch=ch),
        out_shape=jax.ShapeDtypeStruct((2 * r, r * ch), BF16),
        grid=(r // kb,),
        in_specs=[pl.BlockSpec((r, 2 * r), lambda i: (0, 0)),
                  pl.BlockSpec((2 * r, r), lambda i: (0, 0)),
                  pl.BlockSpec((2, kb, r, ch), lambda i: (0, i, 0, 0)),
                  pl.BlockSpec((kb, r, ch), lambda i: (i, 0, order))],
        out_specs=pl.BlockSpec((2 * r, kb * ch), lambda i: (0, i)),
        compiler_params=_params("parallel"),
    )(fb, mc, a4, spec)


def _dft_d_kernel(*refs, nb, ch, gated):
    md_ref, c_ref, u_ref, mul_ref, skip_ref = refs[:5]
    gate_ref = refs[5] if gated else None
    o_ref = refs[-1]
    for j in range(nb):
        sl = slice(j * ch, (j + 1) * ch)
        op = jnp.concatenate([c_ref[0, j], c_ref[1, j]], axis=0)
        y = jnp.dot(md_ref[j], op, preferred_element_type=F32)
        z = mul_ref[:, sl] * (y + u_ref[:, sl] * skip_ref[...])
        if gated:
            z = z * gate_ref[:, sl].astype(F32)
        o_ref[:, sl] = z.astype(o_ref.dtype)


def _dft_stage_d(c2, u2, mul2, skip, gate2, ch):
    r = DFT_R
    md = _const_bf(_dft_mats()[3])
    nb = max(1, min(r, 8192 // ch))
    c4 = c2.reshape(2, r, r, ch)
    blk = pl.BlockSpec((r // 2, nb * ch), lambda i: (0, i))
    specs = [pl.BlockSpec((nb, r // 2, 2 * r), lambda i: (i, 0, 0)),
             pl.BlockSpec((2, nb, r, ch), lambda i: (0, i, 0, 0)),
             blk, blk, pl.BlockSpec((1, ch), lambda i: (0, 0))]
    arrs = [md, c4, u2, mul2, skip]
    if gate2 is not None:
        specs.append(blk)
        arrs.append(gate2)
    return pl.pallas_call(
        functools.partial(_dft_d_kernel, nb=nb, ch=ch, gated=gate2 is not None),
        out_shape=jax.ShapeDtypeStruct((r // 2, r * ch), F32 if gate2 is None else BF16),
        grid=(r // nb,),
        in_specs=specs,
        out_specs=blk,
        compiler_params=_params("parallel"),
    )(*arrs)


def _hyena_long(u, gate, filt, ss, skip, width):
    L = u.shape[0]
    r = DFT_R
    assert 2 * L == r * r
    cw = 2 * width
    spec = _dft_stage_b_spectrum(_dft_stage_a(filt.reshape(r, r * cw), cw), ss, cw)
    lay = lambda t: t.reshape(r // 2, r * width)
    v, x1, x2 = (lay(u[:, k * width:(k + 1) * width]) for k in range(3))
    sk = skip.astype(F32)
    c2 = _dft_stage_bc(_dft_stage_a(v, width), spec, 0, width)
    z = _dft_stage_d(c2, v, x1, sk[0:1], None, width)
    c2 = _dft_stage_bc(_dft_stage_a(z, width), spec, 1, width)
    z = _dft_stage_d(c2, z, x2, sk[1:2], lay(gate), width)
    return z.reshape(L, width)


@functools.lru_cache(maxsize=None)
def _short_dft_mats(L):
    n = 2 * L
    k = np.arange(L)[:, None]
    t = np.arange(n)[None, :]
    f = np.exp(-2j * np.pi * t * (k + 0.5) / n)
    fwd = np.concatenate([f.real, f.imag], axis=0)
    inv = np.conj(f[:, :L]).T * (2.0 / n)
    inv = np.concatenate([inv.real, -inv.imag], axis=1)
    return np.asarray(fwd, np.float32), np.asarray(inv, np.float32)


def _short_spec_kernel(f_ref, k_ref, ss_ref, o_ref):
    o_ref[...] = jnp.dot(f_ref[...], k_ref[...].astype(BF16), preferred_element_type=F32) * lax.rsqrt(ss_ref[...] + EPS)


def _short_conv_kernel(*refs, gated):
    f_ref, i_ref, u_ref, s_ref, mul_ref, skip_ref = refs[:6]
    gate_ref = refs[6] if gated else None
    o_ref = refs[-1]
    u = u_ref[...]
    xs = jnp.dot(f_ref[...], u.astype(BF16), preferred_element_type=F32)
    y = jnp.dot(i_ref[...], _cmul(xs, s_ref[...]).astype(BF16), preferred_element_type=F32)
    z = mul_ref[...] * (y + u * skip_ref[...])
    if gated:
        z = z * gate_ref[...].astype(F32)
    o_ref[...] = z.astype(o_ref.dtype)


def _hyena_short(u, gate, filt, ss, skip, width):
    L = u.shape[0]
    fwd, inv = (_const_bf(m) for m in _short_dft_mats(L))
    cw = 2 * width
    tn = 512
    spec = pl.pallas_call(
        _short_spec_kernel,
        out_shape=jax.ShapeDtypeStruct((2 * L, cw), F32),
        grid=(cw // tn,),
        in_specs=[pl.BlockSpec((2 * L, 2 * L), lambda j: (0, 0)), pl.BlockSpec((2 * L, tn), lambda j: (0, j)),
                  pl.BlockSpec((1, tn), lambda j: (0, j))],
        out_specs=pl.BlockSpec((2 * L, tn), lambda j: (0, j)),
        compiler_params=_params("parallel"),
    )(fwd, filt, ss)

    def conv(uarr, ucol, order, mulcol, gate_arr):
        nt = width // tn
        specs = [pl.BlockSpec((2 * L, L), lambda j: (0, 0)), pl.BlockSpec((L, 2 * L), lambda j: (0, 0)),
                 pl.BlockSpec((L, tn), lambda j: (0, ucol * nt + j)),
                 pl.BlockSpec((2 * L, tn), lambda j: (0, order * nt + j)),
                 pl.BlockSpec((L, tn), lambda j: (0, mulcol * nt + j)),
                 pl.BlockSpec((1, tn), lambda j: (0, order * nt + j))]
        arrs = [fwd[:, :L], inv, uarr, spec, u, skip.astype(F32).reshape(1, cw)]
        if gate_arr is not None:
            specs.append(pl.BlockSpec((L, tn), lambda j: (0, j)))
            arrs.append(gate_arr)
        return pl.pallas_call(
            functools.partial(_short_conv_kernel, gated=gate_arr is not None),
            out_shape=jax.ShapeDtypeStruct((L, width), F32 if gate_arr is None else BF16),
            grid=(nt,),
            in_specs=specs,
            out_specs=pl.BlockSpec((L, tn), lambda j: (0, j)),
            compiler_params=_params("parallel"),
        )(*arrs)

    z = conv(u, 0, 0, 1, None)
    return conv(z, 0, 1, 2, gate)


def _swa_layer(h, hc, w_in, q_g, k_g, sink, need_ctx):
    n, c = h.shape[0], hc.shape[0]
    hq, hk, d = 16, 4, HEAD_DIM
    wq, wk, wv, wg = (_bf(w_in[:, a:b]) for a, b in
                      ((0, hq * d), (hq * d, (hq + hk) * d), ((hq + hk) * d, (hq + 2 * hk) * d),
                       ((hq + 2 * hk) * d, 2 * hq * d + 2 * hk * d)))
    cos, sin = _rope_tables(n, d, d, 0)
    qg, kg = _row(q_g) * (d ** -0.5 * LOG2E), _row(k_g)
    rope = lambda g: [(g, "const"), (cos, "rows"), (sin, "rows")]
    q = _mm(h, wq, _epi_headnorm_rope, rope(qg))
    k = _mm(h, wk, _epi_headnorm_rope, rope(kg))
    v = _mm_t(h, wv, hk)
    g = _mm(h, wg, _epi_silu)
    ck = _mm(hc, wk, _epi_headnorm, [(kg, "const")])
    cv = _mm_t(hc, wv, hk)
    kw = dict(heads=hq, kv_group=hq // hk, maps=1, dqk=d, dv=d, sink=sink)
    o = _attention(q, [(k, None, v), (ck, None, cv)], g, window=SWA_WINDOW, **kw)
    oc = None
    if need_ctx:
        cq = _mm(hc, wq, _epi_headnorm, [(qg, "const")])
        cg = _mm(hc, wg, _epi_silu)
        oc = _attention(cq, [(ck, None, cv)], cg, **kw)
    return o, oc


def _mla_layer(h, hc, w_in, qa_g, kva_g, w_qb, w_kvb, qn_nope_g, qn_pe_g, kn_nope_g, kn_pe_g, need_ctx):
    n, c = h.shape[0], hc.shape[0]
    nh = 16
    qr, kvr, rp, nope = MLA_Q_RANK, MLA_KV_RANK, MLA_ROPE, MLA_NOPE
    w_cq = _bf(w_in[:, :qr])
    w_ckv = _bf(w_in[:, qr:qr + kvr])
    w_kr = _bf(jnp.pad(w_in[:, qr + kvr:qr + kvr + rp], ((0, 0), (0, LANES - rp))))
    w_g = _bf(w_in[:, qr + kvr + rp:])
    wqb = jnp.pad(w_qb.reshape(qr, nh, nope + rp), ((0, 0), (0, 0), (0, 2 * LANES - nope - rp)))
    wqb = _bf(wqb.reshape(qr, nh * 2 * LANES))
    wkv = w_kvb.reshape(kvr, nh, 2 * LANES)
    w_kn, w_v = _bf(wkv[:, :, :nope].reshape(kvr, nh * nope)), _bf(wkv[:, :, nope:].reshape(kvr, nh * LANES))
    pad64 = lambda g: jnp.pad(_row(g), ((0, 0), (0, LANES - rp)))
    gqp, gkp = pad64(qn_pe_g), pad64(kn_pe_g)
    cos, sin = _rope_tables(n, rp, LANES, 0)
    cos_c = jnp.pad(jnp.ones((c, rp), F32), ((0, 0), (0, LANES - rp)))
    sin_c = jnp.zeros((c, LANES), F32)

    def side(hh, cs, sn, queries):
        ckv = _mm(hh, w_ckv, _epi_rownorm, [(_row(kva_g), "row")], tn=kvr)
        kp = _mm(hh, w_kr, _epi_mla_pe, [(gkp, "const"), (cs, "rows"), (sn, "rows")])
        kn = _mm(ckv, w_kn, _epi_headnorm, [(_row(kn_nope_g), "const")])
        v = _mm_t(ckv, w_v, nh)
        if not queries:
            return kn, kp, v, None, None
        cq = _mm(hh, w_cq, _epi_rownorm, [(_row(qa_g), "row")], tn=qr)
        qs = (nope + rp) ** -0.5 * LOG2E
        q = _mm(cq, wqb, _epi_mla_q,
                [(_row(qn_nope_g) * qs, "const"), (gqp * qs, "const"), (cs, "rows"), (sn, "rows")])
        g = _mm(hh, w_g, _epi_silu)
        return kn, kp, v, q, g

    kn, kp, v, q, g = side(h, cos, sin, True)
    ckn, ckp, cv, cq, cg = side(hc, cos_c, sin_c, need_ctx)
    kw = dict(heads=nh, kv_group=1, maps=1, dqk=2 * LANES, dv=LANES)
    o = _attention(q, [(kn, kp, v), (ckn, ckp, cv)], g, **kw)
    oc = _attention(cq, [(ckn, ckp, cv)], cg, **kw) if need_ctx else None
    return o, oc


def _hyena_layer(h, hc, w_in, conv_w, conv_b, f_w1, f_b1, f_w2, f_b2, f_freq, f_w3, skip, need_ctx):
    width = skip.shape[1]
    w_u, w_g = _bf(w_in[:, :3 * width]), _bf(w_in[:, 3 * width:])

    def branch(hh, long):
        L = hh.shape[0]
        u = _conv3(_mm(hh, w_u, _epi_plain, out_dtype=F32), conv_w, conv_b)
        g = _mm(hh, w_g, _epi_silu)
        filt, ss = _hyena_filter(L, f_w1, f_b1, f_w2, f_b2, f_freq, f_w3, width)
        return (_hyena_long if long else _hyena_short)(u, g, filt, ss, skip, width)

    o = branch(h, True)
    oc = branch(hc, False) if need_ctx else None
    return o, oc


def _diff_layer(h, hc, w_in, q_g, k_g, lq1, lk1, lq2, lk2, subln_g, lam_init, need_ctx):
    n, c = h.shape[0], hc.shape[0]
    nh, d = 8, HEAD_DIM
    w = nh * 2 * d
    wq, wk, wv, wg = (_bf(w_in[:, i * w:(i + 1) * w]) for i in range(4))
    cos, sin = _rope_tables(n, d, d, 0)
    qg, kg = _row(q_g) * (d ** -0.5 * LOG2E), _row(k_g)
    rope = lambda g: [(g, "const"), (cos, "rows"), (sin, "rows")]
    q = _mm(h, wq, _epi_headnorm_rope, rope(qg))
    k = _mm(h, wk, _epi_headnorm_rope, rope(kg))
    v = _mm_t(h, wv, nh)
    g = _mm(h, wg, _epi_silu)
    ck = _mm(hc, wk, _epi_headnorm, [(kg, "const")])
    cv = _mm_t(hc, wv, nh)
    lam_vecs = jnp.stack([lq1, lk1, lq2, lk2]).astype(F32)
    kw = dict(heads=nh, kv_group=1, maps=2, dqk=d, dv=2 * d, diff=(lam_init, lam_vecs, _row(subln_g)))
    o = _attention(q, [(k, None, v), (ck, None, cv)], g, **kw)
    oc = None
    if need_ctx:
        cq = _mm(hc, wq, _epi_headnorm, [(qg, "const")])
        cg = _mm(hc, wg, _epi_silu)
        oc = _attention(cq, [(ck, None, cv)], cg, **kw)
    return o, oc


def kernel(x, c, ctx, c_ctx, norm_g, ada_w, ada_b, swa_w_in, swa_q_g, swa_k_g, swa_sink, swa_w_out, mla_w_in, mla_qa_g, mla_kva_g, mla_w_qb, mla_w_kvb, mla_qn_nope_g, mla_qn_pe_g, mla_kn_nope_g, mla_kn_pe_g, mla_w_out, hyena_w_in, hyena_conv_w, hyena_conv_b, hyena_f_w1, hyena_f_b1, hyena_f_w2, hyena_f_b2, hyena_f_freq, hyena_f_w3, hyena_skip, hyena_w_out, diff_w_in, diff_q_g, diff_k_g, diff_lq1, diff_lk1, diff_lq2, diff_lk2, diff_subln_g, diff_w_out):
    depth, d = norm_g.shape
    assert x.shape[0] == 1
    xs, cs = x[0], ctx[0]
    mods = _adaln(c, c_ctx, ada_w, ada_b)
    for i in range(depth):
        kind, j = i % 4, i // 4
        need_ctx = i < depth - 1
        shift, scale, gate = (mods[i, 0, a * d:(a + 1) * d] for a in range(3))
        shift_c, scale_c, gate_c = (mods[i, 1, a * d:(a + 1) * d] for a in range(3))
        h = _modnorm(xs, norm_g[i], scale, shift)
        hc = _modnorm(cs, norm_g[i], scale_c, shift_c)
        if kind == 0:
            o, oc = _swa_layer(h, hc, swa_w_in[j], swa_q_g[j], swa_k_g[j], swa_sink[j], need_ctx)
            w_out = swa_w_out[j]
        elif kind == 1:
            o, oc = _mla_layer(h, hc, mla_w_in[j], mla_qa_g[j], mla_kva_g[j], mla_w_qb[j], mla_w_kvb[j],
                               mla_qn_nope_g[j], mla_qn_pe_g[j], mla_kn_nope_g[j], mla_kn_pe_g[j], need_ctx)
            w_out = mla_w_out[j]
        elif kind == 2:
            o, oc = _hyena_layer(h, hc, hyena_w_in[j], hyena_conv_w[j], hyena_conv_b[j], hyena_f_w1[j],
                                 hyena_f_b1[j], hyena_f_w2[j], hyena_f_b2[j], hyena_f_freq[j], hyena_f_w3[j],
                                 hyena_skip[j], need_ctx)
            w_out = hyena_w_out[j]
        else:
            lam_init = 0.8 - 0.6 * math.exp(-0.3 * i)
            o, oc = _diff_layer(h, hc, diff_w_in[j], diff_q_g[j], diff_k_g[j], diff_lq1[j], diff_lk1[j],
                                diff_lq2[j], diff_lk2[j], diff_subln_g[j], lam_init, need_ctx)
            w_out = diff_w_out[j]
        wo = _bf(w_out)
        xs = _mm(o, wo, _epi_resid, [(xs, "tile"), (_row(gate), "row")], out_dtype=F32)
        if need_ctx:
            cs = _mm(oc, wo, _epi_resid, [(cs, "tile"), (_row(gate_c), "row")], out_dtype=F32)
    return xs[None]
```

```python
import functools
import math

import numpy as np
import jax
import jax.numpy as jnp
from jax import lax
from jax.experimental import pallas as pl
from jax.experimental.pallas import tpu as pltpu

F32 = jnp.float32
BF16 = jnp.bfloat16

EPS = 1e-6
NEG_INF = -1e30
LOG2E = 1.4426950408889634
LANES = 128
VMEM_LIMIT_BYTES = 56 * 1024 * 1024

GRID_W = 64
ROPE_BASE = 10000.0
HEAD_DIM = 128
SWA_WINDOW = 128
MLA_Q_RANK, MLA_KV_RANK, MLA_NOPE, MLA_ROPE = 512, 256, 128, 64
HYENA_BANDS = 16
HYENA_DECAY_TARGET, HYENA_FAST_DECAY, HYENA_SLOW_DECAY = 1e-2, 0.3, 1.5
DFT_R = 128
ATTN_TK = 256
ATTN_TQ = 512
MM_ROW_CHUNK = 256


def _params(*sem):
    return pltpu.CompilerParams(dimension_semantics=sem, vmem_limit_bytes=VMEM_LIMIT_BYTES)


def _row(v):
    return v.reshape(1, -1).astype(F32)


def _bf(w):
    return w.astype(BF16)


def _const_bf(m):
    return jnp.asarray(m, F32).astype(BF16)


def _adaln_kernel(c_ref, w_ref, b_ref, o_ref):
    cv = c_ref[...]
    cv = cv * jax.nn.sigmoid(cv)
    w = w_ref[0]
    r0 = jnp.sum(w * cv[:, 0:1], axis=0, keepdims=True)
    r1 = jnp.sum(w * cv[:, 1:2], axis=0, keepdims=True)
    o_ref[0] = jnp.concatenate([r0, r1], axis=0) + b_ref[0]


def _adaln(c, c_ctx, ada_w, ada_b):
    depth, d, n3 = ada_w.shape
    tn = 512
    cc = jnp.stack([c.reshape(d), c_ctx.reshape(d)], axis=1).astype(F32)
    return pl.pallas_call(
        _adaln_kernel,
        out_shape=jax.ShapeDtypeStruct((depth, 2, n3), F32),
        grid=(depth, n3 // tn),
        in_specs=[pl.BlockSpec((d, 2), lambda l, j: (0, 0)),
                  pl.BlockSpec((1, d, tn), lambda l, j: (l, 0, j)),
                  pl.BlockSpec((1, 1, tn), lambda l, j: (l, 0, j))],
        out_specs=pl.BlockSpec((1, 2, tn), lambda l, j: (l, 0, j)),
        compiler_params=_params("parallel", "parallel"),
    )(cc, ada_w, ada_b.reshape(depth, 1, n3))


def _modnorm_kernel(x_ref, g_ref, sc_ref, sh_ref, o_ref):
    x = x_ref[...]
    y = x * lax.rsqrt(jnp.mean(x * x, axis=-1, keepdims=True) + EPS) * g_ref[...]
    o_ref[...] = (y * (1.0 + sc_ref[...]) + sh_ref[...]).astype(o_ref.dtype)


def _modnorm(x, g, scale, shift):
    m, d = x.shape
    tm = min(m, 512)
    vec = pl.BlockSpec((1, d), lambda i: (0, 0))
    return pl.pallas_call(
        _modnorm_kernel,
        out_shape=jax.ShapeDtypeStruct((m, d), BF16),
        grid=(m // tm,),
        in_specs=[pl.BlockSpec((tm, d), lambda i: (i, 0)), vec, vec, vec],
        out_specs=pl.BlockSpec((tm, d), lambda i: (i, 0)),
        compiler_params=_params("parallel"),
    )(x, _row(g), _row(scale), _row(shift))


def _rms(y, g, n_real):
    ss = jnp.sum(y * y, axis=-1, keepdims=True) * (1.0 / n_real)
    return y * lax.rsqrt(ss + EPS) * g


def _rope(y, cos, sin_signed, shift):
    up = pltpu.roll(y, LANES - shift, 1)
    dn = pltpu.roll(y, shift, 1)
    lane = lax.broadcasted_iota(jnp.int32, y.shape, 1)
    rot = jnp.where((lane & (2 * shift - 1)) < shift, up, dn)
    return y * cos + rot * sin_signed


def _epi_plain(acc, rs, o_ref):
    o_ref[rs, :] = acc.astype(o_ref.dtype)


def _epi_silu(acc, rs, o_ref):
    o_ref[rs, :] = (acc * jax.nn.sigmoid(acc)).astype(o_ref.dtype)


def _epi_headnorm(acc, rs, g_ref, o_ref):
    for h in range(acc.shape[1] // LANES):
        sl = slice(h * LANES, (h + 1) * LANES)
        o_ref[rs, sl] = _rms(acc[:, sl], g_ref[...], LANES).astype(o_ref.dtype)


def _epi_headnorm_rope(acc, rs, g_ref, cos_ref, sin_ref, o_ref):
    for h in range(acc.shape[1] // LANES):
        sl = slice(h * LANES, (h + 1) * LANES)
        y = _rms(acc[:, sl], g_ref[...], LANES)
        o_ref[rs, sl] = _rope(y, cos_ref[rs, :], sin_ref[rs, :], HEAD_DIM // 4).astype(o_ref.dtype)


def _epi_rownorm(acc, rs, g_ref, o_ref):
    o_ref[rs, :] = _rms(acc, g_ref[...], acc.shape[1]).astype(o_ref.dtype)


def _epi_mla_pe(acc, rs, g_ref, cos_ref, sin_ref, o_ref):
    y = _rms(acc, g_ref[...], MLA_ROPE)
    o_ref[rs, :] = _rope(y, cos_ref[rs, :], sin_ref[rs, :], MLA_ROPE // 4).astype(o_ref.dtype)


def _epi_mla_q(acc, rs, gn_ref, gp_ref, cos_ref, sin_ref, o_ref):
    for h in range(acc.shape[1] // (2 * LANES)):
        a = slice(2 * h * LANES, (2 * h + 1) * LANES)
        b = slice((2 * h + 1) * LANES, (2 * h + 2) * LANES)
        o_ref[rs, a] = _rms(acc[:, a], gn_ref[...], MLA_NOPE).astype(o_ref.dtype)
        y = _rms(acc[:, b], gp_ref[...], MLA_ROPE)
        o_ref[rs, b] = _rope(y, cos_ref[rs, :], sin_ref[rs, :], MLA_ROPE // 4).astype(o_ref.dtype)


def _epi_resid(acc, rs, x_ref, gate_ref, o_ref):
    o_ref[rs, :] = x_ref[rs, :] + gate_ref[...] * acc


def _mm_kernel(*refs, epi):
    a_ref, w_ref = refs[0], refs[1]
    chunk = min(a_ref.shape[0], MM_ROW_CHUNK)
    n = a_ref.shape[0] // chunk

    def product(i):
        return jnp.dot(a_ref[i * chunk:(i + 1) * chunk, :], w_ref[...], preferred_element_type=F32)

    nxt = product(0)
    for i in range(n):
        cur = nxt
        if i + 1 < n:
            nxt = product(i + 1)
        epi(cur, slice(i * chunk, (i + 1) * chunk), *refs[2:])


def _mm(a, w, epi, extras=(), out_dtype=BF16, tn=512):
    m, k = a.shape
    n = w.shape[1]
    tm = min(m, 1024)
    tn = min(tn, n)
    assert m % tm == 0 and n % tn == 0
    specs = [pl.BlockSpec((tm, k), lambda i, j: (i, 0)), pl.BlockSpec((k, tn), lambda i, j: (0, j))]
    arrs = [a, w]
    for arr, kind in extras:
        if kind == "row":
            specs.append(pl.BlockSpec((1, tn), lambda i, j: (0, j)))
        elif kind == "tile":
            specs.append(pl.BlockSpec((tm, tn), lambda i, j: (i, j)))
        elif kind == "rows":
            specs.append(pl.BlockSpec((tm, arr.shape[1]), lambda i, j: (i, 0)))
        else:
            specs.append(pl.BlockSpec(arr.shape, lambda i, j: (0, 0)))
        arrs.append(arr)
    return pl.pallas_call(
        functools.partial(_mm_kernel, epi=epi),
        out_shape=jax.ShapeDtypeStruct((m, n), out_dtype),
        grid=(m // tm, n // tn),
        in_specs=specs,
        out_specs=pl.BlockSpec((tm, tn), lambda i, j: (i, j)),
        compiler_params=_params("parallel", "parallel"),
    )(*arrs)


def _mm_t_kernel(wt_ref, a_ref, o_ref):
    acc = lax.dot_general(wt_ref[...], a_ref[...], (((1,), (1,)), ((), ())), preferred_element_type=F32)
    for t in range(o_ref.shape[1]):
        o_ref[0, t] = acc[:, t * ATTN_TK:(t + 1) * ATTN_TK].astype(o_ref.dtype)


def _mm_t(a, w, heads):
    m, k = a.shape
    dv = w.shape[1] // heads
    tm = min(m, 1024)
    return pl.pallas_call(
        _mm_t_kernel,
        out_shape=jax.ShapeDtypeStruct((heads, m // ATTN_TK, dv, ATTN_TK), BF16),
        grid=(m // tm, heads),
        in_specs=[pl.BlockSpec((dv, k), lambda i, j: (j, 0)), pl.BlockSpec((tm, k), lambda i, j: (i, 0))],
        out_specs=pl.BlockSpec((1, tm // ATTN_TK, dv, ATTN_TK), lambda i, j: (j, i, 0, 0)),
        compiler_params=_params("parallel", "parallel"),
    )(w.T, a)


def _rope_tables(n_tok, rot_dim, pad_to, identity_rows):
    half = rot_dim // 2
    qr = half // 2
    rows = n_tok // GRID_W
    row = np.repeat(np.arange(rows, dtype=np.float64), GRID_W)
    col = np.tile(np.arange(GRID_W, dtype=np.float64), rows)
    inv = 1.0 / (ROPE_BASE ** (np.arange(0, half, 2, dtype=np.float64) / half))
    ar = row[:, None] * inv[None, :]
    ac = col[:, None] * inv[None, :]
    ang = np.concatenate([ar, ar, ac, ac], axis=-1)
    sign = np.where((np.arange(rot_dim) % (2 * qr)) < qr, -1.0, 1.0)
    cos = np.cos(ang)
    sin = np.sin(ang) * sign[None, :]
    cos = np.concatenate([cos, np.ones((identity_rows, rot_dim))], axis=0)
    sin = np.concatenate([sin, np.zeros((identity_rows, rot_dim))], axis=0)
    pad = ((0, 0), (0, pad_to - rot_dim))
    return (jnp.asarray(np.pad(cos, pad), F32), jnp.asarray(np.pad(sin, pad), F32))


class _AttnCfg:
    def __init__(self, maps, dqk, dv, n_src, kx, sink, window, diff_lam_init, tq, tk):
        self.maps, self.dqk, self.dv = maps, dqk, dv
        self.n_src, self.kx, self.sink, self.window = n_src, kx, sink, window
        self.diff_lam_init, self.tq, self.tk = diff_lam_init, tq, tk


def _attn_kernel(*refs, cfg):
    it = iter(refs)
    q_ref = next(it)
    srcs = []
    for _ in range(cfg.n_src):
        k_ref = next(it)
        kx_ref = next(it) if cfg.kx else None
        v_ref = next(it)
        srcs.append((k_ref, kx_ref, v_ref))
    sink_ref = next(it) if cfg.sink else None
    gate_ref = next(it)
    if cfg.diff_lam_init is not None:
        lam_ref, subg_ref = next(it), next(it)
    o_ref = next(it)
    m_ref, l_ref, acc_ref = next(it), next(it), next(it)

    tq, tk, maps, dqk = cfg.tq, cfg.tk, cfg.maps, cfg.dqk
    qi = pl.program_id(1)
    dv = cfg.dv
    ones_rows = jnp.ones((16, tk), BF16)

    if cfg.sink:
        m_ref[...] = jnp.full(m_ref.shape, 1.0, F32) * (sink_ref[0, 0:1, 0:1] * LOG2E)
        l_ref[...] = jnp.ones(l_ref.shape, F32)
    else:
        m_ref[...] = jnp.full(m_ref.shape, NEG_INF, F32)
        l_ref[...] = jnp.zeros(l_ref.shape, F32)
    acc_ref[...] = jnp.zeros(acc_ref.shape, F32)

    def tiles(k_ref, kx_ref, v_ref, kts, band):
        m = [m_ref[mp] for mp in range(maps)]
        l = [l_ref[mp] for mp in range(maps)]
        acc = [acc_ref[mp] for mp in range(maps)]
        def scores(u):
            koff = pl.multiple_of(kts[u] * tk, tk)
            out = []
            for mp in range(maps):
                q = q_ref[:, mp * dqk:(mp + 1) * dqk]
                if kx_ref is None:
                    k = k_ref[pl.ds(koff, tk), mp * dqk:(mp + 1) * dqk]
                else:
                    k = jnp.concatenate([k_ref[pl.ds(koff, tk), :], kx_ref[pl.ds(koff, tk), :]], axis=1)
                out.append(lax.dot_general(k, q, (((1,), (1,)), ((), ())), preferred_element_type=F32))
            return out

        s_next = scores(0)
        for u, kt in enumerate(kts):
            s_cur = s_next
            if u + 1 < len(kts):
                s_next = scores(u + 1)
            vt = jnp.concatenate([v_ref[0, kt], ones_rows], axis=0)
            for mp in range(maps):
                s = s_cur[mp]
                if band is not None:
                    kpos = band[u][0] * tk + lax.broadcasted_iota(jnp.int32, (tk, tq), 0)
                    qpos = qi * tq + lax.broadcasted_iota(jnp.int32, (tk, tq), 1)
                    dlt = kpos - qpos
                    s = jnp.where((dlt >= -band[u][1]) & (dlt <= band[u][1]), s, NEG_INF)
                m_new = jnp.maximum(m[mp], jnp.max(s, axis=0, keepdims=True))
                alpha = jnp.exp2(m[mp] - m_new)
                p = jnp.exp2(s - m_new)
                pv = jnp.dot(vt, p.astype(BF16), preferred_element_type=F32)
                l[mp] = alpha * l[mp] + pv[dv:dv + 1]
                acc[mp] = alpha * acc[mp] + pv[:dv]
                m[mp] = m_new
        for mp in range(maps):
            m_ref[mp], l_ref[mp], acc_ref[mp] = m[mp], l[mp], acc[mp]

    for si, (k_ref, kx_ref, v_ref) in enumerate(srcs):
        nk = k_ref.shape[0] // tk
        if cfg.window is not None and si == 0:
            kts, band = [], []
            for u in range(tq // tk + 2):
                raw = qi * (tq // tk) - 1 + u
                inside = (raw >= 0) & (raw < nk)
                kts.append(jnp.clip(raw, 0, nk - 1))
                band.append((raw, jnp.where(inside, cfg.window, -1)))
            tiles(k_ref, kx_ref, v_ref, kts, band)
        else:
            unroll = max(u for u in (8 // maps, 4 // maps, 2, 1) if nk % u == 0)

            def body(j, carry, refs=(k_ref, kx_ref, v_ref), unroll=unroll):
                tiles(*refs, [j * unroll + u for u in range(unroll)], None)
                return carry

            lax.fori_loop(0, nk // unroll, body, 0)

    gate = gate_ref[...].astype(F32)
    if cfg.diff_lam_init is None:
        o = (acc_ref[0] / l_ref[0]).T
    else:
        lam = (jnp.exp(jnp.sum(lam_ref[0:1, :] * lam_ref[1:2, :], axis=-1, keepdims=True))
               - jnp.exp(jnp.sum(lam_ref[2:3, :] * lam_ref[3:4, :], axis=-1, keepdims=True))
               + cfg.diff_lam_init)
        o = (acc_ref[0] / l_ref[0] - lam * (acc_ref[1] / l_ref[1])).T
        o = _rms(o, subg_ref[...], cfg.dv) * (1.0 - cfg.diff_lam_init)
    o_ref[...] = (o * gate).astype(o_ref.dtype)


def _attention(q, srcs, gate, *, heads, kv_group, maps, dqk, dv, sink=None, window=None, diff=None):
    assert window is None or sink is not None
    mq = q.shape[0]
    tq = min(mq, ATTN_TQ)
    tk = ATTN_TK
    kx = srcs[0][1] is not None
    cfg = _AttnCfg(maps, dqk, dv, len(srcs), kx, sink is not None, window,
                   None if diff is None else diff[0], tq, tk)
    kw = maps * dqk - (LANES if kx else 0)
    specs = [pl.BlockSpec((tq, maps * dqk), lambda h, i: (i, h))]
    arrs = [q]
    for k, kxa, v in srcs:
        nk = k.shape[0]
        assert nk % tk == 0
        specs.append(pl.BlockSpec((nk, kw), lambda h, i: (0, h // kv_group)))
        arrs.append(k)
        if kx:
            specs.append(pl.BlockSpec((nk, LANES), lambda h, i: (0, 0)))
            arrs.append(kxa)
        specs.append(pl.BlockSpec((1, nk // tk, dv, tk), lambda h, i: (h // kv_group, 0, 0, 0)))
        arrs.append(v)
    if sink is not None:
        specs.append(pl.BlockSpec((1, 8, LANES), lambda h, i: (h, 0, 0)))
        arrs.append(jnp.broadcast_to(sink.astype(F32)[:, None, None], (heads, 8, LANES)))
    specs.append(pl.BlockSpec((tq, dv), lambda h, i: (i, h)))
    arrs.append(gate)
    if diff is not None:
        specs.append(pl.BlockSpec((4, LANES), lambda h, i: (0, 0)))
        arrs.append(diff[1])
        specs.append(pl.BlockSpec((1, dv), lambda h, i: (0, 0)))
        arrs.append(diff[2])
    return pl.pallas_call(
        functools.partial(_attn_kernel, cfg=cfg),
        out_shape=jax.ShapeDtypeStruct((mq, heads * dv), BF16),
        grid=(heads, mq // tq),
        in_specs=specs,
        out_specs=pl.BlockSpec((tq, dv), lambda h, i: (i, h)),
        scratch_shapes=[pltpu.VMEM((maps, 1, tq), F32), pltpu.VMEM((maps, 1, tq), F32),
                        pltpu.VMEM((maps, dv, tq), F32)],
        compiler_params=_params("parallel", "parallel"),
    )(*arrs)


def _conv3_kernel(u_ref, prev_ref, next_ref, w_ref, b_ref, o_ref):
    i = pl.program_id(0)
    tm = u_ref.shape[0]
    x = u_ref[...]
    row = lax.broadcasted_iota(jnp.int32, x.shape, 0)
    prev_row = jnp.where(i > 0, prev_ref[7:8, :], 0.0)
    next_row = jnp.where(i < pl.num_programs(0) - 1, next_ref[0:1, :], 0.0)
    xm = jnp.where(row == 0, prev_row, pltpu.roll(x, 1, 0))
    xp = jnp.where(row == tm - 1, next_row, pltpu.roll(x, tm - 1, 0))
    o_ref[...] = xm * w_ref[0:1, :] + x * w_ref[1:2, :] + xp * w_ref[2:3, :] + b_ref[...]


def _conv3(u, w, b):
    m, n = u.shape
    tm = min(m, 512)
    tn = 512
    nrb = m // 8
    return pl.pallas_call(
        _conv3_kernel,
        out_shape=jax.ShapeDtypeStruct((m, n), F32),
        grid=(m // tm, n // tn),
        in_specs=[pl.BlockSpec((tm, tn), lambda i, j: (i, j)),
                  pl.BlockSpec((8, tn), lambda i, j: (jnp.maximum(i * (tm // 8) - 1, 0), j)),
                  pl.BlockSpec((8, tn), lambda i, j: (jnp.minimum((i + 1) * (tm // 8), nrb - 1), j)),
                  pl.BlockSpec((3, tn), lambda i, j: (0, j)),
                  pl.BlockSpec((1, tn), lambda i, j: (0, j))],
        out_specs=pl.BlockSpec((tm, tn), lambda i, j: (i, j)),
        compiler_params=_params("parallel", "parallel"),
    )(u, u, u, w.astype(F32), _row(b))


def _split_dot(a, b):
    ah = a.astype(BF16)
    al = (a - ah.astype(F32)).astype(BF16)
    bh = b.astype(BF16)
    bl = (b - bh.astype(F32)).astype(BF16)
    d = functools.partial(jnp.dot, preferred_element_type=F32)
    return d(ah, bh) + (d(ah, bl) + d(al, bh))


def _filter_kernel(z_ref, ts_ref, w1_ref, b1_ref, w2_ref, b2_ref, fr_ref, w3_ref, dl_ref, o_ref, ss_ref):
    h = jnp.sin(fr_ref[0:1, :] * (_split_dot(z_ref[...], w1_ref[...]) + b1_ref[...]))
    h = jnp.sin(fr_ref[1:2, :] * (_split_dot(h, w2_ref[...]) + b2_ref[...]))
    hf = _split_dot(h, w3_ref[0])
    kc = hf * jnp.exp(-ts_ref[:, 0:1] * dl_ref[...]) * ts_ref[:, 1:2]
    o_ref[...] = kc

    @pl.when(pl.program_id(0) == 0)
    def _():
        ss_ref[...] = jnp.zeros(ss_ref.shape, F32)

    ss_ref[...] += jnp.sum(kc * kc, axis=0, keepdims=True)


def _filter_tables(L):
    pos = np.arange(L, dtype=np.float32)
    t = (pos / np.float32(max(L - 1, 1))).astype(np.float32)
    bands = np.linspace(1e-4, HYENA_BANDS - 1, HYENA_BANDS, dtype=np.float32)
    ang = (np.float32(2.0 * math.pi / L) * pos[:, None] * bands[None, :]).astype(np.float32)
    z = np.concatenate([t[:, None], np.cos(ang), -np.sin(ang)], axis=-1).astype(np.float32)
    src = np.concatenate([np.arange(L), np.array([0]), np.arange(L - 1, 0, -1)])
    sign = np.concatenate([np.ones(L), np.zeros(1), -np.ones(L - 1)])
    zt = np.zeros((2 * L, LANES), np.float32)
    zt[:, :z.shape[1]] = z[src]
    ts = np.stack([t[src], sign.astype(np.float32)], axis=1)
    return jnp.asarray(zt), jnp.asarray(ts)


def _hyena_filter(L, f_w1, f_b1, f_w2, f_b2, f_freq, f_w3, width):
    zt, ts = _filter_tables(L)
    hid = f_w1.shape[1]
    w1 = jnp.zeros((LANES, hid), F32).at[:f_w1.shape[0]].set(f_w1.astype(F32))
    w3 = f_w3.astype(F32).reshape(hid, 2, 2, width)
    w3 = jnp.stack([w3[:, :, 0].reshape(hid, 2 * width), w3[:, :, 1].reshape(hid, 2 * width)])
    max_decay = math.log(HYENA_DECAY_TARGET) / HYENA_FAST_DECAY
    min_decay = math.log(HYENA_DECAY_TARGET) / HYENA_SLOW_DECAY
    deltas = np.abs(np.linspace(min_decay, max_decay, width, dtype=np.float32))
    dl = jnp.asarray(np.tile(deltas, 2)[None, :], F32)
    tr = min(L, 512)
    nfw = L // tr
    cw = 2 * width
    full = lambda shp: pl.BlockSpec(shp, lambda i: (0,) * len(shp))
    return pl.pallas_call(
        _filter_kernel,
        out_shape=(jax.ShapeDtypeStruct((2 * L, cw), F32), jax.ShapeDtypeStruct((1, cw), F32)),
        grid=(2 * L // tr,),
        in_specs=[pl.BlockSpec((tr, LANES), lambda i: (i, 0)), pl.BlockSpec((tr, 2), lambda i: (i, 0)),
                  full((LANES, hid)), full((1, hid)), full((hid, hid)), full((1, hid)), full((2, hid)),
                  pl.BlockSpec((1, hid, cw), lambda i: (i // nfw, 0, 0)), full((1, cw))],
        out_specs=(pl.BlockSpec((tr, cw), lambda i: (i, 0)), full((1, cw))),
        compiler_params=_params("arbitrary"),
    )(zt, ts, w1, _row(f_b1), f_w2.astype(F32), _row(f_b2), f_freq.astype(F32), w3, dl)


@functools.lru_cache(maxsize=None)
def _dft_mats():
    r = DFT_R
    n = r * r
    k1 = np.arange(r)[:, None]
    n1 = np.arange(r)[None, :]
    base = np.exp(-2j * np.pi * n1 * (k1 + 0.5) / r)
    n2 = np.arange(r)[:, None, None]
    tw = np.exp(-2j * np.pi * n2 * (k1[None] + 0.5) / n)
    ma = tw * base[None]
    ma = np.concatenate([ma.real, ma.imag], axis=1)
    k2 = np.arange(r // 2)[:, None]
    fb = np.exp(-2j * np.pi * np.arange(r)[None, :] * k2 / r)
    fb = np.block([[fb.real, -fb.imag], [fb.imag, fb.real]])
    mc = np.conj(fb[: r // 2, :r] + 1j * fb[r // 2:, :r]).T
    mc = np.block([[mc.real, -mc.imag], [mc.imag, mc.real]])
    md = np.conj(tw * base[None]).transpose(0, 2, 1)[:, : r // 2, :] * (2.0 / n)
    md = np.concatenate([md.real, -md.imag], axis=2)
    return tuple(np.asarray(m, np.float32) for m in (ma, fb, mc, md))


def _dft_a_kernel(m_ref, x_ref, o_ref, *, nb, ch):
    for j in range(nb):
        xs = x_ref[:, j * ch:(j + 1) * ch].astype(BF16)
        o_ref[:, j * ch:(j + 1) * ch] = jnp.dot(m_ref[j], xs, preferred_element_type=F32).astype(BF16)


def _dft_stage_a(x2, ch):
    r = DFT_R
    kdim = x2.shape[0]
    ma = _const_bf(_dft_mats()[0][:, :, :kdim])
    nb = max(1, min(r, 16384 // ch))
    return pl.pallas_call(
        functools.partial(_dft_a_kernel, nb=nb, ch=ch),
        out_shape=jax.ShapeDtypeStruct((2 * r, r * ch), BF16),
        grid=(r // nb,),
        in_specs=[pl.BlockSpec((nb, 2 * r, kdim), lambda i: (i, 0, 0)),
                  pl.BlockSpec((kdim, nb * ch), lambda i: (0, i))],
        out_specs=pl.BlockSpec((2 * r, nb * ch), lambda i: (0, i)),
        compiler_params=_params("parallel"),
    )(ma, x2)


def _dft_b_kernel(fb_ref, a_ref, ss_ref, o_ref, *, kb):
    rn = lax.rsqrt(ss_ref[...] + EPS)
    for j in range(kb):
        op = jnp.concatenate([a_ref[0, j], a_ref[1, j]], axis=0)
        o_ref[j] = jnp.dot(fb_ref[...], op, preferred_element_type=F32) * rn


def _dft_stage_b_spectrum(a2, ss, ch):
    r = DFT_R
    fb = _const_bf(_dft_mats()[1])
    kb = max(1, min(r, 8192 // ch))
    a4 = a2.reshape(2, r, r, ch)
    return pl.pallas_call(
        functools.partial(_dft_b_kernel, kb=kb),
        out_shape=jax.ShapeDtypeStruct((r, r, ch), F32),
        grid=(r // kb,),
        in_specs=[pl.BlockSpec((r, 2 * r), lambda i: (0, 0)),
                  pl.BlockSpec((2, kb, r, ch), lambda i: (0, i, 0, 0)),
                  pl.BlockSpec((1, ch), lambda i: (0, 0))],
        out_specs=pl.BlockSpec((kb, r, ch), lambda i: (i, 0, 0)),
        compiler_params=_params("parallel"),
    )(fb, a4, ss)


def _cmul(x, s):
    h = x.shape[0] // 2
    xr, xi, sr, si = x[:h], x[h:], s[:h], s[h:]
    return jnp.concatenate([xr * sr - xi * si, xr * si + xi * sr], axis=0)


def _dft_bc_kernel(fb_ref, mc_ref, a_ref, s_ref, o_ref, *, kb, ch):
    for j in range(kb):
        op = jnp.concatenate([a_ref[0, j], a_ref[1, j]], axis=0)
        xs = jnp.dot(fb_ref[...], op, preferred_element_type=F32)
        y = _cmul(xs, s_ref[j]).astype(BF16)
        o_ref[:, j * ch:(j + 1) * ch] = jnp.dot(mc_ref[...], y, preferred_element_type=F32).astype(BF16)


def _dft_stage_bc(a2, spec, order, ch):
    r = DFT_R
    fb, mc = (_const_bf(m) for m in _dft_mats()[1:3])
    kb = max(1, min(r, 8192 // ch))
    a4 = a2.reshape(2, r, r, ch)
    return pl.pallas_call(
        functools.partial(_dft_bc_kernel, kb=kb, ch=ch),
        out_shape=jax.ShapeDtypeStruct((2 * r, r * ch), BF16),
        grid=(r // kb,),
        in_specs=[pl.BlockSpec((r, 2 * r), lambda i: (0, 0)),
                  pl.BlockSpec((2 * r, r), lambda i: (0, 0)),
                  pl.BlockSpec((2, kb, r, ch), lambda i: (0, i, 0, 0)),
                  pl.BlockSpec((kb, r, ch), lambda i: (i, 0, order))],
        out_specs=pl.BlockSpec((2 * r, kb * ch), lambda i: (0, i)),
        compiler_params=_params("parallel"),
    )(fb, mc, a4, spec)


def _dft_d_kernel(*refs, nb, ch, gated):
    md_ref, c_ref, u_ref, mul_ref, skip_ref = refs[:5]
    gate_ref = refs[5] if gated else None
    o_ref = refs[-1]
    for j in range(nb):
        sl = slice(j * ch, (j + 1) * ch)
        op = jnp.concatenate([c_ref[0, j], c_ref[1, j]], axis=0)
        y = jnp.dot(md_ref[j], op, preferred_element_type=F32)
        z = mul_ref[:, sl] * (y + u_ref[:, sl] * skip_ref[...])
        if gated:
            z = z * gate_ref[:, sl].astype(F32)
        o_ref[:, sl] = z.astype(o_ref.dtype)


def _dft_stage_d(c2, u2, mul2, skip, gate2, ch):
    r = DFT_R
    md = _const_bf(_dft_mats()[3])
    nb = max(1, min(r, 8192 // ch))
    c4 = c2.reshape(2, r, r, ch)
    blk = pl.BlockSpec((r // 2, nb * ch), lambda i: (0, i))
    specs = [pl.BlockSpec((nb, r // 2, 2 * r), lambda i: (i, 0, 0)),
             pl.BlockSpec((2, nb, r, ch), lambda i: (0, i, 0, 0)),
             blk, blk, pl.BlockSpec((1, ch), lambda i: (0, 0))]
    arrs = [md, c4, u2, mul2, skip]
    if gate2 is not None:
        specs.append(blk)
        arrs.append(gate2)
    return pl.pallas_call(
        functools.partial(_dft_d_kernel, nb=nb, ch=ch, gated=gate2 is not None),
        out_shape=jax.ShapeDtypeStruct((r // 2, r * ch), F32 if gate2 is None else BF16),
        grid=(r // nb,),
        in_specs=specs,
        out_specs=blk,
        compiler_params=_params("parallel"),
    )(*arrs)


def _hyena_long(u, gate, filt, ss, skip, width):
    L = u.shape[0]
    r = DFT_R
    assert 2 * L == r * r
    cw = 2 * width
    spec = _dft_stage_b_spectrum(_dft_stage_a(filt.reshape(r, r * cw), cw), ss, cw)
    lay = lambda t: t.reshape(r // 2, r * width)
    v, x1, x2 = (lay(u[:, k * width:(k + 1) * width]) for k in range(3))
    sk = skip.astype(F32)
    c2 = _dft_stage_bc(_dft_stage_a(v, width), spec, 0, width)
    z = _dft_stage_d(c2, v, x1, sk[0:1], None, width)
    c2 = _dft_stage_bc(_dft_stage_a(z, width), spec, 1, width)
    z = _dft_stage_d(c2, z, x2, sk[1:2], lay(gate), width)
    return z.reshape(L, width)


@functools.lru_cache(maxsize=None)
def _short_dft_mats(L):
    n = 2 * L
    k = np.arange(L)[:, None]
    t = np.arange(n)[None, :]
    f = np.exp(-2j * np.pi * t * (k + 0.5) / n)
    fwd = np.concatenate([f.real, f.imag], axis=0)
    inv = np.conj(f[:, :L]).T * (2.0 / n)
    inv = np.concatenate([inv.real, -inv.imag], axis=1)
    return np.asarray(fwd, np.float32), np.asarray(inv, np.float32)


def _short_spec_kernel(f_ref, k_ref, ss_ref, o_ref):
    o_ref[...] = jnp.dot(f_ref[...], k_ref[...].astype(BF16), preferred_element_type=F32) * lax.rsqrt(ss_ref[...] + EPS)


def _short_conv_kernel(*refs, gated):
    f_ref, i_ref, u_ref, s_ref, mul_ref, skip_ref = refs[:6]
    gate_ref = refs[6] if gated else None
    o_ref = refs[-1]
    u = u_ref[...]
    xs = jnp.dot(f_ref[...], u.astype(BF16), preferred_element_type=F32)
    y = jnp.dot(i_ref[...], _cmul(xs, s_ref[...]).astype(BF16), preferred_element_type=F32)
    z = mul_ref[...] * (y + u * skip_ref[...])
    if gated:
        z = z * gate_ref[...].astype(F32)
    o_ref[...] = z.astype(o_ref.dtype)


def _hyena_short(u, gate, filt, ss, skip, width):
    L = u.shape[0]
    fwd, inv = (_const_bf(m) for m in _short_dft_mats(L))
    cw = 2 * width
    tn = 512
    spec = pl.pallas_call(
        _short_spec_kernel,
        out_shape=jax.ShapeDtypeStruct((2 * L, cw), F32),
        grid=(cw // tn,),
        in_specs=[pl.BlockSpec((2 * L, 2 * L), lambda j: (0, 0)), pl.BlockSpec((2 * L, tn), lambda j: (0, j)),
                  pl.BlockSpec((1, tn), lambda j: (0, j))],
        out_specs=pl.BlockSpec((2 * L, tn), lambda j: (0, j)),
        compiler_params=_params("parallel"),
    )(fwd, filt, ss)

    def conv(uarr, ucol, order, mulcol, gate_arr):
        nt = width // tn
        specs = [pl.BlockSpec((2 * L, L), lambda j: (0, 0)), pl.BlockSpec((L, 2 * L), lambda j: (0, 0)),
                 pl.BlockSpec((L, tn), lambda j: (0, ucol * nt + j)),
                 pl.BlockSpec((2 * L, tn), lambda j: (0, order * nt + j)),
                 pl.BlockSpec((L, tn), lambda j: (0, mulcol * nt + j)),
                 pl.BlockSpec((1, tn), lambda j: (0, order * nt + j))]
        arrs = [fwd[:, :L], inv, uarr, spec, u, skip.astype(F32).reshape(1, cw)]
        if gate_arr is not None:
            specs.append(pl.BlockSpec((L, tn), lambda j: (0, j)))
            arrs.append(gate_arr)
        return pl.pallas_call(
            functools.partial(_short_conv_kernel, gated=gate_arr is not None),
            out_shape=jax.ShapeDtypeStruct((L, width), F32 if gate_arr is None else BF16),
            grid=(nt,),
            in_specs=specs,
            out_specs=pl.BlockSpec((L, tn), lambda j: (0, j)),
            compiler_params=_params("parallel"),
        )(*arrs)

    z = conv(u, 0, 0, 1, None)
    return conv(z, 0, 1, 2, gate)


def _swa_layer(h, hc, w_in, q_g, k_g, sink, need_ctx):
    n, c = h.shape[0], hc.shape[0]
    hq, hk, d = 16, 4, HEAD_DIM
    wq, wk, wv, wg = (_bf(w_in[:, a:b]) for a, b in
                      ((0, hq * d), (hq * d, (hq + hk) * d), ((hq + hk) * d, (hq + 2 * hk) * d),
                       ((hq + 2 * hk) * d, 2 * hq * d + 2 * hk * d)))
    cos, sin = _rope_tables(n, d, d, 0)
    qg, kg = _row(q_g) * (d ** -0.5 * LOG2E), _row(k_g)
    rope = lambda g: [(g, "const"), (cos, "rows"), (sin, "rows")]
    q = _mm(h, wq, _epi_headnorm_rope, rope(qg))
    k = _mm(h, wk, _epi_headnorm_rope, rope(kg))
    v = _mm_t(h, wv, hk)
    g = _mm(h, wg, _epi_silu)
    ck = _mm(hc, wk, _epi_headnorm, [(kg, "const")])
    cv = _mm_t(hc, wv, hk)
    kw = dict(heads=hq, kv_group=hq // hk, maps=1, dqk=d, dv=d, sink=sink)
    o = _attention(q, [(k, None, v), (ck, None, cv)], g, window=SWA_WINDOW, **kw)
    oc = None
    if need_ctx:
        cq = _mm(hc, wq, _epi_headnorm, [(qg, "const")])
        cg = _mm(hc, wg, _epi_silu)
        oc = _attention(cq, [(ck, None, cv)], cg, **kw)
    return o, oc


def _mla_layer(h, hc, w_in, qa_g, kva_g, w_qb, w_kvb, qn_nope_g, qn_pe_g, kn_nope_g, kn_pe_g, need_ctx):
    n, c = h.shape[0], hc.shape[0]
    nh = 16
    qr, kvr, rp, nope = MLA_Q_RANK, MLA_KV_RANK, MLA_ROPE, MLA_NOPE
    w_cq = _bf(w_in[:, :qr])
    w_ckv = _bf(w_in[:, qr:qr + kvr])
    w_kr = _bf(jnp.pad(w_in[:, qr + kvr:qr + kvr + rp], ((0, 0), (0, LANES - rp))))
    w_g = _bf(w_in[:, qr + kvr + rp:])
    wqb = jnp.pad(w_qb.reshape(qr, nh, nope + rp), ((0, 0), (0, 0), (0, 2 * LANES - nope - rp)))
    wqb = _bf(wqb.reshape(qr, nh * 2 * LANES))
    wkv = w_kvb.reshape(kvr, nh, 2 * LANES)
    w_kn, w_v = _bf(wkv[:, :, :nope].reshape(kvr, nh * nope)), _bf(wkv[:, :, nope:].reshape(kvr, nh * LANES))
    pad64 = lambda g: jnp.pad(_row(g), ((0, 0), (0, LANES - rp)))
    gqp, gkp = pad64(qn_pe_g), pad64(kn_pe_g)
    cos, sin = _rope_tables(n, rp, LANES, 0)
    cos_c = jnp.pad(jnp.ones((c, rp), F32), ((0, 0), (0, LANES - rp)))
    sin_c = jnp.zeros((c, LANES), F32)

    def side(hh, cs, sn, queries):
        ckv = _mm(hh, w_ckv, _epi_rownorm, [(_row(kva_g), "row")], tn=kvr)
        kp = _mm(hh, w_kr, _epi_mla_pe, [(gkp, "const"), (cs, "rows"), (sn, "rows")])
        kn = _mm(ckv, w_kn, _epi_headnorm, [(_row(kn_nope_g), "const")])
        v = _mm_t(ckv, w_v, nh)
        if not queries:
            return kn, kp, v, None, None
        cq = _mm(hh, w_cq, _epi_rownorm, [(_row(qa_g), "row")], tn=qr)
        qs = (nope + rp) ** -0.5 * LOG2E
        q = _mm(cq, wqb, _epi_mla_q,
                [(_row(qn_nope_g) * qs, "const"), (gqp * qs, "const"), (cs, "rows"), (sn, "rows")])
        g = _mm(hh, w_g, _epi_silu)
        return kn, kp, v, q, g

    kn, kp, v, q, g = side(h, cos, sin, True)
    ckn, ckp, cv, cq, cg = side(hc, cos_c, sin_c, need_ctx)
    kw = dict(heads=nh, kv_group=1, maps=1, dqk=2 * LANES, dv=LANES)
    o = _attention(q, [(kn, kp, v), (ckn, ckp, cv)], g, **kw)
    oc = _attention(cq, [(ckn, ckp, cv)], cg, **kw) if need_ctx else None
    return o, oc


def _hyena_layer(h, hc, w_in, conv_w, conv_b, f_w1, f_b1, f_w2, f_b2, f_freq, f_w3, skip, need_ctx):
    width = skip.shape[1]
    w_u, w_g = _bf(w_in[:, :3 * width]), _bf(w_in[:, 3 * width:])

    def branch(hh, long):
        L = hh.shape[0]
        u = _conv3(_mm(hh, w_u, _epi_plain, out_dtype=F32), conv_w, conv_b)
        g = _mm(hh, w_g, _epi_silu)
        filt, ss = _hyena_filter(L, f_w1, f_b1, f_w2, f_b2, f_freq, f_w3, width)
        return (_hyena_long if long else _hyena_short)(u, g, filt, ss, skip, width)

    o = branch(h, True)
    oc = branch(hc, False) if need_ctx else None
    return o, oc


def _diff_layer(h, hc, w_in, q_g, k_g, lq1, lk1, lq2, lk2, subln_g, lam_init, need_ctx):
    n, c = h.shape[0], hc.shape[0]
    nh, d = 8, HEAD_DIM
    w = nh * 2 * d
    wq, wk, wv, wg = (_bf(w_in[:, i * w:(i + 1) * w]) for i in range(4))
    cos, sin = _rope_tables(n, d, d, 0)
    qg, kg = _row(q_g) * (d ** -0.5 * LOG2E), _row(k_g)
    rope = lambda g: [(g, "const"), (cos, "rows"), (sin, "rows")]
    q = _mm(h, wq, _epi_headnorm_rope, rope(qg))
    k = _mm(h, wk, _epi_headnorm_rope, rope(kg))
    v = _mm_t(h, wv, nh)
    g = _mm(h, wg, _epi_silu)
    ck = _mm(hc, wk, _epi_headnorm, [(kg, "const")])
    cv = _mm_t(hc, wv, nh)
    lam_vecs = jnp.stack([lq1, lk1, lq2, lk2]).astype(F32)
    kw = dict(heads=nh, kv_group=1, maps=2, dqk=d, dv=2 * d, diff=(lam_init, lam_vecs, _row(subln_g)))
    o = _attention(q, [(k, None, v), (ck, None, cv)], g, **kw)
    oc = None
    if need_ctx:
        cq = _mm(hc, wq, _epi_headnorm, [(qg, "const")])
        cg = _mm(hc, wg, _epi_silu)
        oc = _attention(cq, [(ck, None, cv)], cg, **kw)
    return o, oc


def kernel(x, c, ctx, c_ctx, norm_g, ada_w, ada_b, swa_w_in, swa_q_g, swa_k_g, swa_sink, swa_w_out, mla_w_in, mla_qa_g, mla_kva_g, mla_w_qb, mla_w_kvb, mla_qn_nope_g, mla_qn_pe_g, mla_kn_nope_g, mla_kn_pe_g, mla_w_out, hyena_w_in, hyena_conv_w, hyena_conv_b, hyena_f_w1, hyena_f_b1, hyena_f_w2, hyena_f_b2, hyena_f_freq, hyena_f_w3, hyena_skip, hyena_w_out, diff_w_in, diff_q_g, diff_k_g, diff_lq1, diff_lk1, diff_lq2, diff_lk2, diff_subln_g, diff_w_out):
    depth, d = norm_g.shape
    assert x.shape[0] == 1
    xs, cs = x[0], ctx[0]
    mods = _adaln(c, c_ctx, ada_w, ada_b)
    for i in range(depth):
        kind, j = i % 4, i // 4
        need_ctx = i < depth - 1
        shift, scale, gate = (mods[i, 0, a * d:(a + 1) * d] for a in range(3))
        shift_c, scale_c, gate_c = (mods[i, 1, a * d:(a + 1) * d] for a in range(3))
        h = _modnorm(xs, norm_g[i], scale, shift)
        hc = _modnorm(cs, norm_g[i], scale_c, shift_c)
        if kind == 0:
            o, oc = _swa_layer(h, hc, swa_w_in[j], swa_q_g[j], swa_k_g[j], swa_sink[j], need_ctx)
            w_out = swa_w_out[j]
        elif kind == 1:
            o, oc = _mla_layer(h, hc, mla_w_in[j], mla_qa_g[j], mla_kva_g[j], mla_w_qb[j], mla_w_kvb[j],
                               mla_qn_nope_g[j], mla_qn_pe_g[j], mla_kn_nope_g[j], mla_kn_pe_g[j], need_ctx)
            w_out = mla_w_out[j]
        elif kind == 2:
            o, oc = _hyena_layer(h, hc, hyena_w_in[j], hyena_conv_w[j], hyena_conv_b[j], hyena_f_w1[j],
                                 hyena_f_b1[j], hyena_f_w2[j], hyena_f_b2[j], hyena_f_freq[j], hyena_f_w3[j],
                                 hyena_skip[j], need_ctx)
            w_out = hyena_w_out[j]
        else:
            lam_init = 0.8 - 0.6 * math.exp(-0.3 * i)
            o, oc = _diff_layer(h, hc, diff_w_in[j], diff_q_g[j], diff_k_g[j], diff_lq1[j], diff_lk1[j],
                                diff_lq2[j], diff_lk2[j], diff_subln_g[j], lam_init, need_ctx)
            w_out = diff_w_out[j]
        wo = _bf(w_out)
        xs = _mm(o, wo, _epi_resid, [(xs, "tile"), (_row(gate), "row")], out_dtype=F32)
        if need_ctx:
            cs = _mm(oc, wo, _epi_resid, [(cs, "tile"), (_row(gate_c), "row")], out_dtype=F32)
    return xs[None]
```
